```python
import math
import jax, jax.numpy as jnp
from jax import lax
import numpy as np

D_MODEL = 1024
BATCH = 32
SEQ = 2048
DEPTH = 1
DEC_BATCH = 8
DEC_SEQ = 16
PAST_LEN = 1024

CHUNK = 64
DK_A = 128
DV_A = 128
H_A = D_MODEL // 128
QK_A = H_A * DK_A
V_A = H_A * DV_A
CONV_W = 4
CONV_CH = 2 * QK_A + V_A
N_B = 64
H_B = D_MODEL // N_B
D_B = H_B * N_B
LORA_W = 64
LORA_A = 64
LORA_G = 128
D_FF = 4 * D_MODEL
COLS_A = CONV_CH + V_A + 2 * H_A
COLS_B = 3 * D_B + LORA_W + LORA_A + LORA_G
COLS_IN = COLS_A + COLS_B + 2 * D_MODEL
DN_ALPHA = (2 * DEPTH) ** 0.25
DN_BETA = (8 * DEPTH) ** -0.25
LN_EPS = 1e-5
NORM_EPS = 1e-6
GN_EPS = 64e-5

kernel_name = 'stream_hybrid_gdn_rwkv7_step'


def _layer_norm(x, g, b):
    xf = x.astype(jnp.float32)
    mu = jnp.mean(xf, -1, keepdims=True)
    var = jnp.mean(jnp.square(xf - mu), -1, keepdims=True)
    return ((xf - mu) * lax.rsqrt(var + LN_EPS) * g + b).astype(x.dtype)


def _l2norm(x):
    return x * lax.rsqrt(jnp.sum(jnp.square(x), -1, keepdims=True) + NORM_EPS)


def _causal_conv(u, buf, w):
    T = u.shape[1]
    full = jnp.concatenate([buf.astype(u.dtype), u], axis=1)
    out = full[:, 0:T] * w[0]
    for j in range(1, CONV_W):
        out = out + full[:, j:j + T] * w[j]
    return jax.nn.silu(out), full[:, T:]


def _gdn_chunked(q, k, v, g, beta, s0):
    Bn, T = q.shape[:2]
    pad = (-T) % CHUNK
    n = (T + pad) // CHUNK

    def blocks(t):
        t = jnp.pad(t, [(0, 0), (0, pad)] + [(0, 0)] * (t.ndim - 2))
        t = t.reshape((Bn, n, CHUNK) + t.shape[2:])
        return jnp.moveaxis(t, (1, 3), (0, 2))

    qb, kb, vb, gb, bb = (blocks(t) for t in (q, k, v, g, beta))
    gc = jnp.cumsum(gb, axis=-1)
    idx = jnp.arange(CHUNK)
    incl = idx[:, None] >= idx[None, :]
    strict = idx[:, None] > idx[None, :]
    decay = jnp.exp(jnp.where(incl, gc[..., :, None] - gc[..., None, :], -jnp.inf))
    kbeta = kb * bb[..., None]
    a_mat = jnp.einsum('nbhik,nbhjk->nbhij', kbeta, kb) * decay * strict
    lhs = a_mat + jnp.eye(CHUNK, dtype=a_mat.dtype)
    rhs = jnp.concatenate([vb * bb[..., None], kbeta * jnp.exp(gc)[..., None]], axis=-1)
    sol = lax.linalg.triangular_solve(lhs, rhs, left_side=True, lower=True, unit_diagonal=True)
    u, w = sol[..., :DV_A], sol[..., DV_A:]
    qk = jnp.einsum('nbhik,nbhjk->nbhij', qb, kb) * decay

    def step(S, inp):
        qi, ki, ui, wi, gi, qki = inp
        v_new = ui - jnp.einsum('bhck,bhkv->bhcv', wi, S)
        o = (jnp.einsum('bhck,bhkv->bhcv', qi * jnp.exp(gi)[..., None], S)
             + jnp.einsum('bhij,bhjv->bhiv', qki, v_new))
        g_last = gi[..., -1:]
        S = (S * jnp.exp(g_last)[..., None]
             + jnp.einsum('bhck,bhcv->bhkv', ki * jnp.exp(g_last - gi)[..., None], v_new))
        return S, o

    s_new, o = lax.scan(step, s0, (qb, kb, u, w, gc, qk))
    o = jnp.moveaxis(o, (0, 2), (1, 3)).reshape(Bn, n * CHUNK, H_A, DV_A)[:, :T]
    return o, s_new


def _gdn_branch(qkv_raw, z, beta_raw, a_raw, conv_buf, s0, conv_a_w, a_log, dt_bias, gdn_norm_g):
    f32 = jnp.float32
    Bn, T, _ = qkv_raw.shape
    qkv, conv_new = _causal_conv(qkv_raw, conv_buf, conv_a_w)
    qkv = qkv.astype(f32)
    q = _l2norm(qkv[..., :QK_A].reshape(Bn, T, H_A, DK_A)) * (DK_A ** -0.5)
    k = _l2norm(qkv[..., QK_A:2 * QK_A].reshape(Bn, T, H_A, DK_A))
    v = qkv[..., 2 * QK_A:].reshape(Bn, T, H_A, DV_A)
    beta = jax.nn.sigmoid(beta_raw.astype(f32))
    g = -jnp.exp(a_log.astype(f32)) * jax.nn.softplus(a_raw.astype(f32) + dt_bias.astype(f32))
    o, s_new = _gdn_chunked(q, k, v, g, beta, s0.astype(f32))
    o = o * lax.rsqrt(jnp.mean(jnp.square(o), -1, keepdims=True) + NORM_EPS) * gdn_norm_g.astype(f32)
    o = o * jax.nn.silu(z.astype(f32)).reshape(Bn, T, H_A, DV_A)
    return o.reshape(Bn, T, V_A).astype(z.dtype), conv_new, s_new.astype(z.dtype)


def _wkv7_scan(r, w, k, v, a, b, s0):
    def step(S, inp):
        rt, wt, kt, vt, at, bt = inp
        sa = jnp.einsum('bhij,bhj->bhi', S, at)
        S = S * wt[:, :, None, :] + sa[..., None] * bt[:, :, None, :] + vt[..., None] * kt[:, :, None, :]
        return S, jnp.einsum('bhij,bhj->bhi', S, rt)

    xs = tuple(jnp.moveaxis(t, 1, 0) for t in (r, w, k, v, a, b))
    s_new, y = lax.scan(step, s0, xs)
    return jnp.moveaxis(y, 0, 1), s_new


def _rwkv7_branch(proj_b, shift_buf, s0, mu_b, w0, w_up, a0, a_up, g_up, k_k, k_a, r_k, lnx_g, lnx_b):
    f32 = jnp.float32
    Bn, T, _ = proj_b.shape
    prev = jnp.concatenate([shift_buf[:, None].astype(proj_b.dtype), proj_b[:, :-1]], axis=1)
    xs = (proj_b + mu_b * (prev - proj_b)).astype(f32)
    r, k, v, wd, ad, gd = jnp.split(
        xs, [D_B, 2 * D_B, 3 * D_B, 3 * D_B + LORA_W, 3 * D_B + LORA_W + LORA_A], axis=-1)
    w_log = -jax.nn.softplus(-(w0 + jnp.tanh(wd) @ w_up)) - 0.5
    decay = jnp.exp(-jnp.exp(w_log))
    a = jax.nn.sigmoid(a0 + ad @ a_up)
    gate = jax.nn.sigmoid(gd) @ g_up

    def heads(t):
        return t.reshape(Bn, T, H_B, N_B)

    kk = _l2norm(heads(k * k_k))
    k = k * (1.0 + (a - 1.0) * k_a)
    r_h, k_h, v_h, a_h = heads(r), heads(k), heads(v), heads(a)
    y, s_new = _wkv7_scan(r_h, heads(decay), k_h, v_h, -kk, kk * a_h, s0.astype(f32))
    mu = jnp.mean(y, -1, keepdims=True)
    var = jnp.mean(jnp.square(y - mu), -1, keepdims=True)
    y = ((y - mu) * lax.rsqrt(var + GN_EPS)).reshape(Bn, T, D_B) * lnx_g + lnx_b
    y = y + (jnp.sum(r_h * k_h * r_k, -1, keepdims=True) * v_h).reshape(Bn, T, D_B)
    return (y * gate).astype(proj_b.dtype), proj_b[:, -1], s_new.astype(proj_b.dtype)


def _layer(x, conv_buf, s_gdn, shift_buf, s_wkv,
           w_in, conv_a_w, a_log, dt_bias, gdn_norm_g, mu_b, w0, w_up, a0, a_up, g_up,
           k_k, k_a, r_k, lnx_g, lnx_b, w_out, ln1_g, ln1_b, w_ff1, w_ff2, ln2_g, ln2_b):
    proj = jnp.einsum('btd,dc->btc', x, w_in)
    qkv_raw, z, beta_raw, a_raw, proj_b, gate_a, gate_b = jnp.split(
        proj, [CONV_CH, CONV_CH + V_A, CONV_CH + V_A + H_A, COLS_A, COLS_A + COLS_B,
               COLS_A + COLS_B + D_MODEL], axis=-1)
    o_a, conv_new, s_gdn_new = _gdn_branch(qkv_raw, z, beta_raw, a_raw, conv_buf, s_gdn,
                                           conv_a_w, a_log, dt_bias, gdn_norm_g)
    o_b, shift_new, s_wkv_new = _rwkv7_branch(proj_b, shift_buf, s_wkv, mu_b, w0, w_up, a0, a_up,
                                              g_up, k_k, k_a, r_k, lnx_g, lnx_b)
    merged = jax.nn.sigmoid(gate_a) * o_a + jax.nn.sigmoid(gate_b) * o_b
    mix = jnp.einsum('btc,cd->btd', merged, w_out)
    h = _layer_norm(DN_ALPHA * x + mix, ln1_g, ln1_b)
    ff = jnp.einsum('btf,fd->btd', jnp.square(jax.nn.relu(jnp.einsum('btd,df->btf', h, w_ff1))), w_ff2)
    y = _layer_norm(DN_ALPHA * h + ff, ln2_g, ln2_b)
    return y, conv_new, s_gdn_new, shift_new, s_wkv_new


def setup_inputs(seed: int = 0) -> dict:
    key = jax.random.key(seed)
    ks = jax.random.split(key, 29)
    f32 = jnp.float32
    L = DEPTH

    def nrm(k, shape, scale):
        return jax.random.normal(k, shape, f32) * scale

    def unif(k, shape, lo, hi):
        return jax.random.uniform(k, shape, f32, minval=lo, maxval=hi)

    dt = jnp.exp(unif(ks[9], (L, H_A), math.log(1e-3), math.log(1e-1)))
    return {
        'x_prompt': nrm(ks[0], (BATCH, SEQ, D_MODEL), 1.0),
        'x_sample': nrm(ks[1], (DEC_BATCH, DEC_SEQ, D_MODEL), 1.0),
        'state_conv_a': nrm(ks[2], (L, DEC_BATCH, CONV_W - 1, CONV_CH), 1.0),
        'state_gdn': nrm(ks[3], (L, DEC_BATCH, H_A, DK_A, DV_A), 0.3),
        'state_shift_b': nrm(ks[4], (L, DEC_BATCH, COLS_B), 1.0),
        'state_wkv': nrm(ks[5], (L, DEC_BATCH, H_B, N_B, N_B), 0.3),
        'w_in': nrm(ks[6], (L, D_MODEL, COLS_IN), D_MODEL ** -0.5),
        'conv_a_w': nrm(ks[7], (L, CONV_W, CONV_CH), CONV_W ** -0.5),
        'a_log': jnp.log(unif(ks[8], (L, H_A), 1.0, 16.0)),
        'dt_bias': dt + jnp.log(-jnp.expm1(-dt)),
        'gdn_norm_g': 1.0 + nrm(ks[10], (L, DV_A), 0.02),
        'mu_b': unif(ks[11], (L, COLS_B), 0.0, 1.0),
        'w0': unif(ks[12], (L, D_B), -4.0, 0.5),
        'w_up': nrm(ks[13], (L, LORA_W, D_B), 0.1 * LORA_W ** -0.5),
        'a0': nrm(ks[14], (L, D_B), 0.1),
        'a_up': nrm(ks[15], (L, LORA_A, D_B), LORA_A ** -0.5),
        'g_up': nrm(ks[16], (L, LORA_G, D_B), LORA_G ** -0.5),
        'k_k': 0.85 + nrm(ks[17], (L, D_B), 0.05),
        'k_a': 1.0 + nrm(ks[18], (L, D_B), 0.05),
        'r_k': nrm(ks[19], (L, H_B, N_B), 0.1),
        'lnx_g': 1.0 + nrm(ks[20], (L, D_B), 0.02),
        'lnx_b': nrm(ks[21], (L, D_B), 0.02),
        'w_out': nrm(ks[22], (L, D_MODEL, D_MODEL), DN_BETA * D_MODEL ** -0.5),
        'ln1_g': 1.0 + nrm(ks[23], (L, D_MODEL), 0.02),
        'ln1_b': nrm(ks[24], (L, D_MODEL), 0.02),
        'w_ff1': nrm(ks[25], (L, D_MODEL, D_FF), D_MODEL ** -0.5),
        'w_ff2': nrm(ks[26], (L, D_FF, D_MODEL), DN_BETA * D_FF ** -0.5),
        'ln2_g': 1.0 + nrm(ks[27], (L, D_MODEL), 0.02),
        'ln2_b': nrm(ks[28], (L, D_MODEL), 0.02),
    }


def reference(x_prompt, x_sample, state_conv_a, state_gdn, state_shift_b, state_wkv,
              w_in, conv_a_w, a_log, dt_bias, gdn_norm_g, mu_b, w0, w_up, a0, a_up, g_up,
              k_k, k_a, r_k, lnx_g, lnx_b, w_out, ln1_g, ln1_b, w_ff1, w_ff2, ln2_g, ln2_b):
    weights = (w_in, conv_a_w, a_log, dt_bias, gdn_norm_g, mu_b, w0, w_up, a0, a_up, g_up,
               k_k, k_a, r_k, lnx_g, lnx_b, w_out, ln1_g, ln1_b, w_ff1, w_ff2, ln2_g, ln2_b)
    bp, dtype = x_prompt.shape[0], x_prompt.dtype
    hp, hs = x_prompt, x_sample
    out_p, out_s = [], []
    for l in range(DEPTH):
        lw = tuple(wt[l] for wt in weights)
        hp, *st_p = _layer(hp,
                           jnp.zeros((bp, CONV_W - 1, CONV_CH), dtype),
                           jnp.zeros((bp, H_A, DK_A, DV_A), dtype),
                           jnp.zeros((bp, COLS_B), dtype),
                           jnp.zeros((bp, H_B, N_B, N_B), dtype),
                           *lw)
        hs, *st_s = _layer(hs, state_conv_a[l], state_gdn[l], state_shift_b[l], state_wkv[l], *lw)
        out_p.append(st_p)
        out_s.append(st_s)
    conv_p, gdn_p, shift_p, wkv_p = (jnp.stack([s[i] for s in out_p]) for i in range(4))
    conv_s, gdn_s, shift_s, wkv_s = (jnp.stack([s[i] for s in out_s]) for i in range(4))
    return (hp, hs, conv_p, gdn_p, shift_p, wkv_p, conv_s, gdn_s, shift_s, wkv_s)
```

```python
import functools

import jax
import jax.numpy as jnp
from jax import lax
from jax.experimental import pallas as pl
from jax.experimental.pallas import tpu as pltpu

F32 = jnp.float32
BF16 = jnp.bfloat16

LANES = 128
HEAD_A = 128
HEAD_B = 64
CONV_W = 4
LN_EPS = 1e-5
NORM_EPS = 1e-6
GN_EPS = 64e-5
CHUNK = 64
VMEM_LIMIT = 56 * 1024 * 1024


def _dot(a, b, dims, exact):
    if exact:
        return lax.dot_general(a.astype(F32), b.astype(F32), (dims, ((), ())),
                               precision=lax.Precision.HIGHEST, preferred_element_type=F32)
    return lax.dot_general(a.astype(BF16), b.astype(BF16), (dims, ((), ())),
                           preferred_element_type=F32)


def _nn(a, b, exact=False):
    return _dot(a, b, ((1,), (0,)), exact)


def _nt(a, b, exact=False):
    return _dot(a, b, ((1,), (1,)), exact)


def _tn(a, b, exact=False):
    return _dot(a, b, ((0,), (0,)), exact)


def _sigmoid(x):
    return 1.0 / (1.0 + jnp.exp(-x))


def _softplus(x):
    return jnp.maximum(x, 0.0) + jnp.log1p(jnp.exp(-jnp.abs(x)))


def _layer_norm(x, g, b):
    mu = jnp.mean(x, -1, keepdims=True)
    xc = x - mu
    var = jnp.mean(xc * xc, -1, keepdims=True)
    return xc * lax.rsqrt(var + LN_EPS) * g + b


def _unit_lower_inverse(x, eye, n_sq, exact):
    t = eye + x
    p = x
    for _ in range(n_sq):
        p = _nn(p, p, exact)
        t = t + _nn(t, p, exact)
    return t


def _tri_masks(c):
    ii = lax.broadcasted_iota(jnp.int32, (c, c), 0)
    jj = lax.broadcasted_iota(jnp.int32, (c, c), 1)
    return ii >= jj, ii > jj, ii == jj


def _proj_kernel(x_ref, w_ref, o_ref, xb_ref):
    @pl.when(pl.program_id(1) == 0)
    def _():
        xb_ref[...] = x_ref[...].astype(BF16)

    o_ref[...] = jnp.dot(xb_ref[...], w_ref[...], preferred_element_type=F32).astype(o_ref.dtype)


def _proj(x2d, w_p, tm, tn):
    n, d = x2d.shape
    cols = w_p.shape[1]
    return pl.pallas_call(
        _proj_kernel,
        grid=(n // tm, cols // tn),
        in_specs=[pl.BlockSpec((tm, d), lambda i, j: (i, 0)),
                  pl.BlockSpec((d, tn), lambda i, j: (0, j))],
        out_specs=pl.BlockSpec((tm, tn), lambda i, j: (i, j)),
        out_shape=jax.ShapeDtypeStruct((n, cols), BF16),
        scratch_shapes=[pltpu.VMEM((tm, d), BF16)],
        compiler_params=pltpu.CompilerParams(
            dimension_semantics=("parallel", "arbitrary"), vmem_limit_bytes=VMEM_LIMIT),
        name="proj",
    )(x2d, w_p)


def _gdn_kernel(q_ref, k_ref, v_ref, z_ref, ba_ref, ga_ref, cq_ref, ck_ref, cv_ref, s0_ref,
                wq_ref, wk_ref, wv_ref, alog_ref, dtb_ref, ng_ref,
                o_ref, sout_ref,
                s_scr, xq_scr, xk_scr, xv_scr, *, chunk, n_chunks):
    h = pl.program_id(1)
    t = pl.program_id(2)
    c_len = chunk
    tc = chunk * n_chunks
    pad = 8
    hist = CONV_W - 1

    @pl.when(t == 0)
    def _():
        s_scr[...] = s0_ref[...]
        xq_scr[pad - hist:pad, :] = cq_ref[...]
        xk_scr[pad - hist:pad, :] = ck_ref[...]
        xv_scr[pad - hist:pad, :] = cv_ref[...]

    def conv(x_ref, xs, w_ref):
        xs[pad:pad + tc, :] = x_ref[...].astype(F32)
        w = w_ref[...]
        acc = xs[pad - hist:pad - hist + tc, :] * w[0:1, :]
        for j in range(1, CONV_W):
            acc = acc + xs[pad - hist + j:pad - hist + j + tc, :] * w[j:j + 1, :]
        xs[pad - hist:pad, :] = xs[pad + tc - hist:pad + tc, :]
        return acc * _sigmoid(acc)

    qc = conv(q_ref, xq_scr, wq_ref)
    kc = conv(k_ref, xk_scr, wk_ref)
    v = conv(v_ref, xv_scr, wv_ref)
    q = qc * lax.rsqrt(jnp.sum(qc * qc, -1, keepdims=True) + NORM_EPS) * (HEAD_A ** -0.5)
    k = kc * lax.rsqrt(jnp.sum(kc * kc, -1, keepdims=True) + NORM_EPS)

    ba = ba_ref[...].astype(F32)
    n_heads = pl.num_programs(1)
    lane = lax.broadcasted_iota(jnp.int32, (1, LANES), 1)
    beta_all = _sigmoid(ba)
    g_all = -jnp.exp(alog_ref[...]) * _softplus(ba + dtb_ref[...])
    beta_t = jnp.sum(jnp.where(lane == h, beta_all, 0.0), -1, keepdims=True)

    incl, strict, eye_m = _tri_masks(c_len)
    eye = eye_m.astype(F32)
    ltri = incl.astype(F32)
    n_sq = max(c_len.bit_length() - 2, 0)

    per_chunk = []
    for c in range(n_chunks):
        sl = slice(c * c_len, (c + 1) * c_len)
        gcum = _nn(ltri, g_all[sl], exact=True)
        gc = jnp.sum(jnp.where(lane == h + n_heads, gcum, 0.0), -1, keepdims=True)
        gc_row = jnp.sum(jnp.where(eye_m, gc, 0.0), 0, keepdims=True)
        decay = jnp.where(incl, jnp.exp(jnp.where(incl, gc - gc_row, 0.0)), 0.0)
        beta = beta_t[sl]
        kb = k[sl] * beta
        a_mat = jnp.where(strict, _nt(kb, k[sl]) * decay, 0.0)
        t_inv = _unit_lower_inverse(-a_mat, eye, n_sq, exact=True)
        eg = jnp.exp(gc)
        sol = _nn(t_inv, jnp.concatenate([v[sl] * beta, kb * eg], -1))
        u, w = sol[:, :HEAD_A], sol[:, HEAD_A:]
        qk = _nt(q[sl], k[sl]) * decay
        g_last = gc[c_len - 1:c_len, :]
        per_chunk.append((u, w, qk, q[sl] * eg, k[sl] * jnp.exp(g_last - gc), jnp.exp(g_last)))

    s = s_scr[...]
    outs = []
    for u, w, qk, qg, kd, e_last in per_chunk:
        ws = _nn(jnp.concatenate([w, qg], 0), s)
        v_new = u - ws[:c_len]
        outs.append(ws[c_len:] + _nn(qk, v_new))
        s = s * e_last + _tn(kd, v_new)
    s_scr[...] = s
    o = outs[0] if n_chunks == 1 else jnp.concatenate(outs, 0)

    o = o * lax.rsqrt(jnp.mean(o * o, -1, keepdims=True) + NORM_EPS) * ng_ref[...]
    z = z_ref[...].astype(F32)
    o = o * (z * _sigmoid(z))
    o_ref[...] = (_sigmoid(ga_ref[...].astype(F32)) * o).astype(o_ref.dtype)

    @pl.when(t == pl.num_programs(2) - 1)
    def _():
        sout_ref[...] = s


def _gdn(proj, conv_state, s0, conv_w, alog_row, dtb_row, norm_g, cb, tc, chunk):
    b, t_len, _ = proj.shape
    n_heads = s0.shape[1]
    n_t = t_len // tc

    def pcol(off):
        return pl.BlockSpec((None, tc, LANES), lambda bi, hi, ti: (bi, ti, hi + off))

    def ccol(off):
        return pl.BlockSpec((None, CONV_W - 1, LANES), lambda bi, hi, ti: (bi, 0, hi + off))

    def wcol(off):
        return pl.BlockSpec((CONV_W, LANES), lambda bi, hi, ti: (0, hi + off))

    row = pl.BlockSpec((1, LANES), lambda bi, hi, ti: (0, 0))
    state = pl.BlockSpec((None, None, HEAD_A, HEAD_A), lambda bi, hi, ti: (bi, hi, 0, 0))
    kern = functools.partial(_gdn_kernel, chunk=chunk, n_chunks=tc // chunk)
    return pl.pallas_call(
        kern,
        grid=(b, n_heads, n_t),
        in_specs=[pcol(cb["q"]), pcol(cb["k"]), pcol(cb["v"]), pcol(cb["z"]),
                  pl.BlockSpec((None, tc, LANES), lambda bi, hi, ti: (bi, ti, cb["ba"])),
                  pcol(cb["gate_a"]),
                  ccol(0), ccol(n_heads), ccol(2 * n_heads), state,
                  wcol(0), wcol(n_heads), wcol(2 * n_heads), row, row, row],
        out_specs=[pl.BlockSpec((None, tc, LANES), lambda bi, hi, ti: (bi, ti, hi)), state],
        out_shape=[jax.ShapeDtypeStruct((b, t_len, n_heads * HEAD_A), BF16),
                   jax.ShapeDtypeStruct(s0.shape, F32)],
        scratch_shapes=[pltpu.VMEM((HEAD_A, HEAD_A), F32)] + [pltpu.VMEM((tc + 8, LANES), F32)] * 3,
        compiler_params=pltpu.CompilerParams(
            dimension_semantics=("parallel", "parallel", "arbitrary"), vmem_limit_bytes=VMEM_LIMIT),
        name="gdn",
    )(proj, proj, proj, proj, proj, proj, conv_state, conv_state, conv_state, s0,
      conv_w, conv_w, conv_w, alog_row, dtb_row, norm_g)


def _rwkv_kernel(r_ref, k_ref, v_ref, l_ref, g_ref, gb_ref,
                 shr_ref, shk_ref, shv_ref, shl_ref, shg_ref,
                 mur_ref, muk_ref, muv_ref, mul_ref, mug_ref,
                 w0_ref, a0_ref, kk_ref, ka_ref, rk_ref, lng_ref, lnb_ref,
                 wup_ref, aup_ref, gup_ref, s0_ref,
                 o_ref, sout_ref,
                 s_scr, xr_scr, xk_scr, xv_scr, xl_scr, xg_scr, *, chunk, n_chunks):
    t = pl.program_id(2)
    c_len = chunk
    tc = chunk * n_chunks
    pad = 8
    nb = HEAD_B
    heads = LANES // nb

    @pl.when(t == 0)
    def _():
        s_scr[...] = s0_ref[...]
        xr_scr[pad - 1:pad, :] = shr_ref[...]
        xk_scr[pad - 1:pad, :] = shk_ref[...]
        xv_scr[pad - 1:pad, :] = shv_ref[...]
        xl_scr[pad - 1:pad, :] = shl_ref[...]
        xg_scr[pad - 1:pad, :] = shg_ref[...]

    def shift_mix(x_ref, xs, mu_ref):
        x = x_ref[...].astype(F32)
        xs[pad:pad + tc, :] = x
        prev = xs[pad - 1:pad - 1 + tc, :]
        xs[pad - 1:pad, :] = xs[pad + tc - 1:pad + tc, :]
        return x + mu_ref[...] * (prev - x)

    xr = shift_mix(r_ref, xr_scr, mur_ref)
    xk = shift_mix(k_ref, xk_scr, muk_ref)
    xv = shift_mix(v_ref, xv_scr, muv_ref)
    xl = shift_mix(l_ref, xl_scr, mul_ref)
    xg = shift_mix(g_ref, xg_scr, mug_ref)

    lora_w = wup_ref.shape[0]
    w_log = -_softplus(-(w0_ref[...] + _nn(jnp.tanh(xl[:, :lora_w]), wup_ref[...]))) - 0.5
    logw = -jnp.exp(w_log)
    a = _sigmoid(a0_ref[...] + _nn(xl[:, lora_w:], aup_ref[...]))
    gate = _nn(_sigmoid(xg), gup_ref[...])

    lane = lax.broadcasted_iota(jnp.int32, (1, LANES), 1)
    head_of_lane = lane // nb

    def per_head_sum(x):
        out = jnp.zeros_like(x)
        for hh in range(heads):
            m = head_of_lane == hh
            out = jnp.where(m, jnp.sum(jnp.where(m, x, 0.0), -1, keepdims=True), out)
        return out

    kkv = xk * kk_ref[...]
    kk = kkv * lax.rsqrt(per_head_sum(kkv * kkv) + NORM_EPS)
    k2 = xk * (1.0 + (a - 1.0) * ka_ref[...])
    av = -kk
    bv = kk * a
    bonus = per_head_sum(xr * k2 * rk_ref[...]) * xv

    incl, strict, eye_m = _tri_masks(c_len)
    eye = eye_m.astype(F32)
    ltri = incl.astype(F32)
    n_sq = max(c_len.bit_length() - 2, 0)

    per_chunk = []
    for c in range(n_chunks):
        sl = slice(c * c_len, (c + 1) * c_len)
        lw = logw[sl]
        cl = _nn(ltri, lw, exact=True)
        cl_last = cl[c_len - 1:c_len, :]
        e_neg = jnp.exp(-cl)
        rt = xr[sl] * jnp.exp(cl)
        kt = k2[sl] * e_neg
        bt = bv[sl] * e_neg
        at = av[sl] * jnp.exp(cl - lw)
        e_tail = jnp.exp(cl_last - cl)
        kbar = k2[sl] * e_tail
        bbar = bv[sl] * e_tail
        w_all = jnp.exp(cl_last)
        v_c = xv[sl]
        heads_c = []
        for hh in range(heads):
            ls = slice(hh * nb, (hh + 1) * nb)
            gram = _nt(jnp.concatenate([at[:, ls], rt[:, ls]], 0),
                       jnp.concatenate([bt[:, ls], kt[:, ls]], 0), exact=True)
            ab = jnp.where(strict, gram[:c_len, :c_len], 0.0)
            ak = jnp.where(strict, gram[:c_len, c_len:], 0.0)
            rb = jnp.where(incl, gram[c_len:, :c_len], 0.0)
            rk = jnp.where(incl, gram[c_len:, c_len:], 0.0)
            t_inv = _unit_lower_inverse(ab, eye, n_sq, exact=True)
            sol = _nn(t_inv, jnp.concatenate([at[:, ls], _nn(ak, v_c[:, ls], exact=True)], -1), exact=True)
            heads_c.append((sol[:, :nb], sol[:, nb:], rt[:, ls], jnp.concatenate([rb, rk], -1),
                            v_c[:, ls], jnp.concatenate([bbar[:, ls], kbar[:, ls]], 0), w_all[:, ls]))
        per_chunk.append(heads_c)

    ys = []
    for hh in range(heads):
        s = s_scr[hh]
        y_h = []
        for c in range(n_chunks):
            wbar, u, rt_h, rbk, v_h, bk_bar, w_h = per_chunk[c][hh]
            ws = _nt(jnp.concatenate([wbar, rt_h], 0), s, exact=True)
            pm = u + ws[:c_len]
            pv = jnp.concatenate([pm, v_h], 0)
            y_h.append(ws[c_len:] + _nn(rbk, pv, exact=True))
            s = s * w_h + _tn(pv, bk_bar, exact=True)
        s_scr[hh] = s
        y = y_h[0] if n_chunks == 1 else jnp.concatenate(y_h, 0)
        mu = jnp.mean(y, -1, keepdims=True)
        yc = y - mu
        var = jnp.mean(yc * yc, -1, keepdims=True)
        ys.append(yc * lax.rsqrt(var + GN_EPS))
    y = jnp.concatenate(ys, -1) * lng_ref[...] + lnb_ref[...] + bonus
    o_ref[...] = (_sigmoid(gb_ref[...].astype(F32)) * (y * gate)).astype(o_ref.dtype)

    @pl.when(t == pl.num_programs(2) - 1)
    def _():
        sout_ref[...] = s_scr[...]


def _rwkv(proj, shift_state, s0, mu_b, w0, a0, k_k, k_a, r_k, lnx_g, lnx_b, w_up, a_up, g_up,
          cb, tc, chunk):
    b, t_len, _ = proj.shape
    n_heads = s0.shape[1]
    heads = LANES // HEAD_B
    n_pairs = n_heads // heads
    n_t = t_len // tc
    lora_blk = 3 * n_pairs

    def pcol(off, fixed=False):
        if fixed:
            return pl.BlockSpec((None, tc, LANES), lambda bi, pi, ti: (bi, ti, off))
        return pl.BlockSpec((None, tc, LANES), lambda bi, pi, ti: (bi, ti, pi + off))

    def scol(off, fixed=False):
        if fixed:
            return pl.BlockSpec((None, 1, LANES), lambda bi, pi, ti: (bi, 0, off))
        return pl.BlockSpec((None, 1, LANES), lambda bi, pi, ti: (bi, 0, pi + off))

    def mcol(off, fixed=False):
        if fixed:
            return pl.BlockSpec((1, LANES), lambda bi, pi, ti: (0, off))
        return pl.BlockSpec((1, LANES), lambda bi, pi, ti: (0, pi + off))

    prow = pl.BlockSpec((1, LANES), lambda bi, pi, ti: (0, pi))
    state = pl.BlockSpec((None, heads, HEAD_B, HEAD_B), lambda bi, pi, ti: (bi, pi, 0, 0))
    kern = functools.partial(_rwkv_kernel, chunk=chunk, n_chunks=tc // chunk)
    rb = cb["b"]
    return pl.pallas_call(
        kern,
        grid=(b, n_pairs, n_t),
        in_specs=[pcol(rb), pcol(rb + n_pairs), pcol(rb + 2 * n_pairs),
                  pcol(rb + lora_blk, True), pcol(rb + lora_blk + 1, True), pcol(cb["gate_b"]),
                  scol(0), scol(n_pairs), scol(2 * n_pairs), scol(lora_blk, True), scol(lora_blk + 1, True),
                  mcol(0), mcol(n_pairs), mcol(2 * n_pairs), mcol(lora_blk, True), mcol(lora_blk + 1, True),
                  prow, prow, prow, prow, prow, prow, prow,
                  pl.BlockSpec((w_up.shape[0], LANES), lambda bi, pi, ti: (0, pi)),
                  pl.BlockSpec((a_up.shape[0], LANES), lambda bi, pi, ti: (0, pi)),
                  pl.BlockSpec((g_up.shape[0], LANES), lambda bi, pi, ti: (0, pi)),
                  state],
        out_specs=[pl.BlockSpec((None, tc, LANES), lambda bi, pi, ti: (bi, ti, pi)), state],
        out_shape=[jax.ShapeDtypeStruct((b, t_len, n_heads * HEAD_B), BF16),
                   jax.ShapeDtypeStruct(s0.shape, F32)],
        scratch_shapes=[pltpu.VMEM((heads, HEAD_B, HEAD_B), F32)] + [pltpu.VMEM((tc + 8, LANES), F32)] * 5,
        compiler_params=pltpu.CompilerParams(
            dimension_semantics=("parallel", "parallel", "arbitrary"), vmem_limit_bytes=VMEM_LIMIT),
        name="rwkv",
    )(proj, proj, proj, proj, proj, proj,
      shift_state, shift_state, shift_state, shift_state, shift_state,
      mu_b, mu_b, mu_b, mu_b, mu_b,
      w0, a0, k_k, k_a, r_k, lnx_g, lnx_b, w_up, a_up, g_up, s0)


def _mix_ffn_kernel(x_ref, ma_ref, mb_ref, wo_ref, g1_ref, b1_ref, w1_ref, w2_ref, g2_ref, b2_ref,
                    y_ref, *, alpha, ff_chunk):
    merged = ma_ref[...] + mb_ref[...]
    mix = jnp.dot(merged, wo_ref[...], preferred_element_type=F32)
    h = _layer_norm(alpha * x_ref[...] + mix, g1_ref[...], b1_ref[...])
    hb = h.astype(BF16)
    d_ff = w1_ref.shape[1]
    ff = jnp.zeros_like(h)
    for c in range(d_ff // ff_chunk):
        cs = slice(c * ff_chunk, (c + 1) * ff_chunk)
        a = jnp.maximum(jnp.dot(hb, w1_ref[:, cs], preferred_element_type=F32), 0.0)
        ff = ff + jnp.dot((a * a).astype(BF16), w2_ref[cs, :], preferred_element_type=F32)
    y_ref[...] = _layer_norm(alpha * h + ff, g2_ref[...], b2_ref[...])


def _mix_ffn(x2d, ma, mb, w_out, ln1_g, ln1_b, w_ff1, w_ff2, ln2_g, ln2_b, alpha, tm):
    n, d = x2d.shape
    d_ff = w_ff1.shape[1]

    def const(shape):
        return pl.BlockSpec(shape, lambda i: (0, 0), pipeline_mode=pl.Buffered(1))

    tile = pl.BlockSpec((tm, d), lambda i: (i, 0))
    kern = functools.partial(_mix_ffn_kernel, alpha=alpha, ff_chunk=min(d_ff, 1024))
    return pl.pallas_call(
        kern,
        grid=(n // tm,),
        in_specs=[tile, tile, tile, const((d, d)), const((1, d)), const((1, d)),
                  const((d, d_ff)), const((d_ff, d)), const((1, d)), const((1, d))],
        out_specs=tile,
        out_shape=jax.ShapeDtypeStruct((n, d), F32),
        compiler_params=pltpu.CompilerParams(
            dimension_semantics=("parallel",), vmem_limit_bytes=VMEM_LIMIT),
        name="mix_ffn",
    )(x2d, ma, mb, w_out, ln1_g, ln1_b, w_ff1, w_ff2, ln2_g, ln2_b)


def _pick_tile(n, target):
    t = min(n, target)
    while n % t:
        t //= 2
    return t


def _layer(x, conv_buf, s_gdn, shift_buf, s_wkv, p):
    b, t_len, d = x.shape
    n = b * t_len
    cb = p["cb"]
    x2d = x.reshape(n, d)

    proj = _proj(x2d, p["w_in"], _pick_tile(n, 1024), p["proj_tn"]).reshape(b, t_len, -1)

    chunk = min(CHUNK, t_len)
    tc = _pick_tile(t_len, 256)
    ma, s_gdn_new = _gdn(proj, conv_buf, s_gdn, p["conv_w"], p["alog_row"], p["dtb_row"],
                         p["gdn_norm_g"], cb, tc, chunk)
    mb, s_wkv_new = _rwkv(proj, shift_buf[:, None, :], s_wkv, p["mu_b"], p["w0"], p["a0"], p["k_k"],
                          p["k_a"], p["r_k"], p["lnx_g"], p["lnx_b"], p["w_up"], p["a_up"], p["g_up"],
                          cb, tc, chunk)

    y = _mix_ffn(x2d, ma.reshape(n, -1), mb.reshape(n, -1), p["w_out"], p["ln1_g"], p["ln1_b"],
                 p["w_ff1"], p["w_ff2"], p["ln2_g"], p["ln2_b"], p["alpha"], _pick_tile(n, 512))

    conv_cols = conv_buf.shape[-1]
    hist = CONV_W - 1
    assert t_len >= hist
    conv_new = proj[:, t_len - hist:, :conv_cols].astype(x.dtype)
    b0 = cb["b"] * LANES
    shift_new = proj[:, t_len - 1, b0:b0 + shift_buf.shape[-1]].astype(x.dtype)
    return y.reshape(b, t_len, d), conv_new, s_gdn_new, shift_new, s_wkv_new


def _prep_layer(l, depth, w_in, conv_a_w, a_log, dt_bias, gdn_norm_g, mu_b, w0, w_up, a0, a_up, g_up,
                k_k, k_a, r_k, lnx_g, lnx_b, w_out, ln1_g, ln1_b, w_ff1, w_ff2, ln2_g, ln2_b):
    n_heads_a = a_log.shape[1]
    conv_ch = conv_a_w.shape[2]
    d_model = w_out.shape[1]
    v_a = n_heads_a * HEAD_A
    cols_b = mu_b.shape[1]
    small0 = conv_ch + v_a
    small1 = small0 + 2 * n_heads_a
    w = w_in[l]
    w_p = jnp.concatenate(
        [w[:, :small0], jnp.pad(w[:, small0:small1], ((0, 0), (0, LANES - 2 * n_heads_a))), w[:, small1:]],
        axis=1).astype(BF16)
    blk = lambda c: c // LANES
    cb = {"q": 0, "k": blk(n_heads_a * HEAD_A), "v": blk(2 * n_heads_a * HEAD_A), "z": blk(conv_ch),
          "ba": blk(small0), "b": blk(small0) + 1}
    cb["gate_a"] = cb["b"] + blk(cols_b)
    cb["gate_b"] = cb["gate_a"] + blk(d_model)
    n_blocks = w_p.shape[1] // LANES
    tn_blocks = max(f for f in range(1, 17) if n_blocks % f == 0)

    def lane_row(vec, offset):
        return jnp.zeros((1, LANES), F32).at[0, offset:offset + vec.shape[0]].set(vec)

    row = lambda a: a[l].reshape(1, -1)
    return {
        "cb": cb, "proj_tn": tn_blocks * LANES, "w_in": w_p, "conv_w": conv_a_w[l],
        "alog_row": lane_row(a_log[l], n_heads_a), "dtb_row": lane_row(dt_bias[l], n_heads_a),
        "gdn_norm_g": row(gdn_norm_g), "mu_b": row(mu_b), "w0": row(w0), "a0": row(a0),
        "k_k": row(k_k), "k_a": row(k_a), "r_k": row(r_k), "lnx_g": row(lnx_g), "lnx_b": row(lnx_b),
        "w_up": w_up[l].astype(BF16), "a_up": a_up[l].astype(BF16), "g_up": g_up[l].astype(BF16),
        "w_out": w_out[l].astype(BF16), "ln1_g": row(ln1_g), "ln1_b": row(ln1_b),
        "w_ff1": w_ff1[l].astype(BF16), "w_ff2": w_ff2[l].astype(BF16),
        "ln2_g": row(ln2_g), "ln2_b": row(ln2_b), "alpha": float((2 * depth) ** 0.25),
    }


def kernel(x_prompt, x_sample, state_conv_a, state_gdn, state_shift_b, state_wkv, w_in, conv_a_w, a_log, dt_bias, gdn_norm_g, mu_b, w0, w_up, a0, a_up, g_up, k_k, k_a, r_k, lnx_g, lnx_b, w_out, ln1_g, ln1_b, w_ff1, w_ff2, ln2_g, ln2_b):
    weights = (w_in, conv_a_w, a_log, dt_bias, gdn_norm_g, mu_b, w0, w_up, a0, a_up, g_up,
               k_k, k_a, r_k, lnx_g, lnx_b, w_out, ln1_g, ln1_b, w_ff1, w_ff2, ln2_g, ln2_b)
    depth = w_in.shape[0]
    bp, dtype = x_prompt.shape[0], x_prompt.dtype
    hp, hs = x_prompt, x_sample
    out_p, out_s = [], []
    for l in range(depth):
        p = _prep_layer(l, depth, *weights)
        zeros = lambda ref: jnp.zeros((bp,) + ref.shape[2:], dtype)
        hp, *st_p = _layer(hp, zeros(state_conv_a), zeros(state_gdn), zeros(state_shift_b),
                           zeros(state_wkv), p)
        hs, *st_s = _layer(hs, state_conv_a[l], state_gdn[l], state_shift_b[l], state_wkv[l], p)
        out_p.append(st_p)
        out_s.append(st_s)
    conv_p, gdn_p, shift_p, wkv_p = (jnp.stack([s[i] for s in out_p]) for i in range(4))
    conv_s, gdn_s, shift_s, wkv_s = (jnp.stack([s[i] for s in out_s]) for i in range(4))
    return (hp, hs, conv_p, gdn_p, shift_p, wkv_p, conv_s, gdn_s, shift_s, wkv_s)
```

```python
import functools

import jax
import jax.numpy as jnp
from jax import lax
from jax.experimental import pallas as pl
from jax.experimental.pallas import tpu as pltpu

F32 = jnp.float32
BF16 = jnp.bfloat16

LANES = 128
HEAD_A = 128
HEAD_B = 64
CONV_W = 4
LN_EPS = 1e-5
NORM_EPS = 1e-6
GN_EPS = 64e-5
CHUNK = 64
GDN_HEADS_PER_STEP = 8
RWKV_PAIRS_PER_STEP = 8
TIME_TILE = 128
VMEM_LIMIT = 56 * 1024 * 1024


def _dot(a, b, dims):
    return lax.dot_general(a.astype(BF16), b.astype(BF16), (dims, ((), ())), preferred_element_type=F32)


def _nn(a, b):
    return _dot(a, b, ((1,), (0,)))


def _nt(a, b):
    return _dot(a, b, ((1,), (1,)))


def _tn(a, b):
    return _dot(a, b, ((0,), (0,)))


def _sigmoid(x):
    return 1.0 / (1.0 + jnp.exp(-x))


def _softplus(x):
    return jnp.maximum(x, 0.0) + jnp.log1p(jnp.exp(-jnp.abs(x)))


def _layer_norm(x, g, b):
    mu = jnp.mean(x, -1, keepdims=True)
    xc = x - mu
    var = jnp.mean(xc * xc, -1, keepdims=True)
    return xc * lax.rsqrt(var + LN_EPS) * g + b


def _tri_masks(c):
    ii = lax.broadcasted_iota(jnp.int32, (c, c), 0)
    jj = lax.broadcasted_iota(jnp.int32, (c, c), 1)
    return ii >= jj, ii > jj, ii == jj


def _proj_kernel(x_ref, w_ref, o_ref, xb_ref):
    @pl.when(pl.program_id(1) == 0)
    def _():
        xb_ref[...] = x_ref[...].astype(BF16)

    o_ref[...] = jnp.dot(xb_ref[...], w_ref[...], preferred_element_type=F32).astype(o_ref.dtype)


def _proj(x2d, w_p, tm, tn):
    n, d = x2d.shape
    cols = w_p.shape[1]
    return pl.pallas_call(
        _proj_kernel,
        grid=(n // tm, cols // tn),
        in_specs=[pl.BlockSpec((tm, d), lambda i, j: (i, 0)),
                  pl.BlockSpec((d, tn), lambda i, j: (0, j))],
        out_specs=pl.BlockSpec((tm, tn), lambda i, j: (i, j)),
        out_shape=jax.ShapeDtypeStruct((n, cols), BF16),
        scratch_shapes=[pltpu.VMEM((tm, d), BF16)],
        compiler_params=pltpu.CompilerParams(
            dimension_semantics=("parallel", "arbitrary"), vmem_limit_bytes=VMEM_LIMIT),
        name="proj",
    )(x2d, w_p)


def _gdn_kernel(q_ref, k_ref, v_ref, z_ref, ga_ref, ba_ref, cq_ref, ck_ref, cv_ref, s0_ref,
                wq_ref, wk_ref, wv_ref, alog_ref, dtb_ref, ng_ref,
                o_ref, sout_ref,
                s_scr, xq_scr, xk_scr, xv_scr, *, chunk, n_chunks, hb):
    hg = pl.program_id(1)
    t = pl.program_id(2)
    c_len = chunk
    tc = chunk * n_chunks
    pad = 8
    hist = CONV_W - 1

    @pl.when(t == 0)
    def _():
        s_scr[...] = s0_ref[...]
        xq_scr[pad - hist:pad, :] = cq_ref[...]
        xk_scr[pad - hist:pad, :] = ck_ref[...]
        xv_scr[pad - hist:pad, :] = cv_ref[...]

    def conv(x_ref, xs, w_ref):
        xs[pad:pad + tc, :] = x_ref[...].astype(F32)
        w = w_ref[...]
        acc = xs[pad - hist:pad - hist + tc, :] * w[0:1, :]
        for j in range(1, CONV_W):
            acc = acc + xs[pad - hist + j:pad - hist + j + tc, :] * w[j:j + 1, :]
        xs[pad - hist:pad, :] = xs[pad + tc - hist:pad + tc, :]
        return acc * _sigmoid(acc)

    qc = conv(q_ref, xq_scr, wq_ref)
    kc = conv(k_ref, xk_scr, wk_ref)
    vc = conv(v_ref, xv_scr, wv_ref)

    ba = ba_ref[...].astype(F32)
    n_heads = pl.num_programs(1) * hb
    lane = lax.broadcasted_iota(jnp.int32, (1, LANES), 1)
    beta_all = _sigmoid(ba)
    g_all = -jnp.exp(alog_ref[...]) * _softplus(ba + dtb_ref[...])

    incl, strict, eye_m = _tri_masks(c_len)
    upper = (lax.broadcasted_iota(jnp.int32, (c_len, c_len), 0)
             <= lax.broadcasted_iota(jnp.int32, (c_len, c_len), 1))
    eye = eye_m.astype(F32)
    n_sq = max(c_len.bit_length() - 2, 0)

    chains = [(hl, c) for hl in range(hb) for c in range(n_chunks)]
    q_c, k_c, v_c, beta_c, g_c = [], [], [], [], []
    for hl in range(hb):
        h = hg * hb + hl
        ls = slice(hl * HEAD_A, (hl + 1) * HEAD_A)
        qh, kh = qc[:, ls], kc[:, ls]
        q = qh * lax.rsqrt(jnp.sum(qh * qh, -1, keepdims=True) + NORM_EPS) * (HEAD_A ** -0.5)
        k = kh * lax.rsqrt(jnp.sum(kh * kh, -1, keepdims=True) + NORM_EPS)
        beta_t = jnp.sum(jnp.where(lane == h, beta_all, 0.0), -1, keepdims=True)
        g_t = jnp.sum(jnp.where(lane == h + n_heads, g_all, 0.0), -1, keepdims=True)
        for c in range(n_chunks):
            sl = slice(c * c_len, (c + 1) * c_len)
            q_c.append(q[sl]); k_c.append(k[sl]); v_c.append(vc[sl, ls])
            beta_c.append(beta_t[sl]); g_c.append(g_t[sl])

    g_row = [jnp.sum(jnp.where(eye_m, g, 0.0), 0, keepdims=True) for g in g_c]
    gc = [jnp.sum(jnp.where(incl, g, 0.0), 1, keepdims=True) for g in g_row]
    gc_row = [jnp.sum(jnp.where(upper, g, 0.0), 0, keepdims=True) for g in g_c]
    decay = [jnp.where(incl, jnp.exp(jnp.where(incl, a - b, 0.0)), 0.0) for a, b in zip(gc, gc_row)]
    kb = [k * b for k, b in zip(k_c, beta_c)]
    gram = [_nt(jnp.concatenate([a, q], 0), k) for a, q, k in zip(kb, q_c, k_c)]
    qk = [g[c_len:] * d for g, d in zip(gram, decay)]
    p_pow = [jnp.where(strict, -g[:c_len] * d, 0.0) for g, d in zip(gram, decay)]
    t_inv = [eye + x for x in p_pow]
    for _ in range(n_sq):
        p_pow = [_nn(x, x) for x in p_pow]
        t_inv = [t_ + _nn(t_, x) for t_, x in zip(t_inv, p_pow)]
    eg = [jnp.exp(g) for g in gc]
    sol = [_nn(t_, jnp.concatenate([v * b, a * e], -1))
           for t_, v, b, a, e in zip(t_inv, v_c, beta_c, kb, eg)]
    wq = [jnp.concatenate([s_[:, HEAD_A:], q * e], 0) for s_, q, e in zip(sol, q_c, eg)]
    g_last = [g[c_len - 1:c_len, :] for g in gc]
    kd = [k * jnp.exp(gl - g) for k, gl, g in zip(k_c, g_last, gc)]
    e_last = [jnp.exp(g) for g in g_last]

    s = [s_scr[hl] for hl in range(hb)]
    outs = [[] for _ in range(hb)]
    for c in range(n_chunks):
        idx = [chains.index((hl, c)) for hl in range(hb)]
        ws = [_nn(wq[i], s_) for i, s_ in zip(idx, s)]
        v_new = [sol[i][:, :HEAD_A] - w_[:c_len] for i, w_ in zip(idx, ws)]
        for hl, (i, w_, vn) in enumerate(zip(idx, ws, v_new)):
            outs[hl].append(w_[c_len:] + _nn(qk[i], vn))
        s = [s_ * e_last[i] + _tn(kd[i], vn) for i, s_, vn in zip(idx, s, v_new)]

    for hl in range(hb):
        ls = slice(hl * HEAD_A, (hl + 1) * HEAD_A)
        s_scr[hl] = s[hl]
        o = outs[hl][0] if n_chunks == 1 else jnp.concatenate(outs[hl], 0)
        o = o * lax.rsqrt(jnp.mean(o * o, -1, keepdims=True) + NORM_EPS) * ng_ref[...]
        z = z_ref[:, ls].astype(F32)
        o = o * (z * _sigmoid(z))
        o_ref[:, ls] = (_sigmoid(ga_ref[:, ls].astype(F32)) * o).astype(o_ref.dtype)

    @pl.when(t == pl.num_programs(2) - 1)
    def _():
        sout_ref[...] = s_scr[...]


def _gdn(proj, conv_state, s0, conv_w, alog_row, dtb_row, norm_g, cb, tc, chunk, hb):
    b, t_len, _ = proj.shape
    n_heads = s0.shape[1]
    n_t = t_len // tc
    wb = hb * HEAD_A
    grp = n_heads // hb

    def pcol(blk):
        return pl.BlockSpec((None, tc, wb), lambda bi, hi, ti: (bi, ti, hi + blk // hb))

    def ccol(g):
        return pl.BlockSpec((None, CONV_W - 1, wb), lambda bi, hi, ti: (bi, 0, hi + g * grp))

    def wcol(g):
        return pl.BlockSpec((CONV_W, wb), lambda bi, hi, ti: (0, hi + g * grp))

    row = pl.BlockSpec((1, LANES), lambda bi, hi, ti: (0, 0))
    state = pl.BlockSpec((None, hb, HEAD_A, HEAD_A), lambda bi, hi, ti: (bi, hi, 0, 0))
    kern = functools.partial(_gdn_kernel, chunk=chunk, n_chunks=tc // chunk, hb=hb)
    return pl.pallas_call(
        kern,
        grid=(b, grp, n_t),
        in_specs=[pcol(cb["q"]), pcol(cb["k"]), pcol(cb["v"]), pcol(cb["z"]), pcol(cb["gate_a"]),
                  pl.BlockSpec((None, tc, LANES), lambda bi, hi, ti: (bi, ti, cb["ba"])),
                  ccol(0), ccol(1), ccol(2), state,
                  wcol(0), wcol(1), wcol(2), row, row, row],
        out_specs=[pl.BlockSpec((None, tc, wb), lambda bi, hi, ti: (bi, ti, hi)), state],
        out_shape=[jax.ShapeDtypeStruct((b, t_len, n_heads * HEAD_A), BF16),
                   jax.ShapeDtypeStruct(s0.shape, F32)],
        scratch_shapes=[pltpu.VMEM((hb, HEAD_A, HEAD_A), F32)] + [pltpu.VMEM((tc + 8, wb), F32)] * 3,
        compiler_params=pltpu.CompilerParams(
            dimension_semantics=("parallel", "parallel", "arbitrary"), vmem_limit_bytes=VMEM_LIMIT),
        name="gdn",
    )(proj, proj, proj, proj, proj, proj, conv_state, conv_state, conv_state, s0,
      conv_w, conv_w, conv_w, alog_row, dtb_row, norm_g)


def _cumsum_rows(ltri, x):
    n = x.shape[1]
    hi = x.astype(BF16)
    r1 = x - hi.astype(F32)
    mid = r1.astype(BF16)
    lo = (r1 - mid.astype(F32)).astype(BF16)
    res = jnp.dot(ltri, jnp.concatenate([hi, mid, lo], -1), preferred_element_type=F32)
    return res[:, :n] + res[:, n:2 * n] + res[:, 2 * n:]


def _rwkv_kernel(r_ref, k_ref, v_ref, gb_ref, l_ref, g_ref,
                 shr_ref, shk_ref, shv_ref, shl_ref, shg_ref,
                 mur_ref, muk_ref, muv_ref, mul_ref, mug_ref,
                 w0_ref, a0_ref, kk_ref, ka_ref, rk_ref, lng_ref, lnb_ref,
                 wup_ref, aup_ref, gup_ref, s0_ref,
                 o_ref, sout_ref,
                 s_scr, xr_scr, xk_scr, xv_scr, xl_scr, xg_scr, *, chunk, n_chunks, pp):
    t = pl.program_id(2)
    c_len = chunk
    c2 = 2 * chunk
    tc = chunk * n_chunks
    pad = 8
    nb = HEAD_B

    @pl.when(t == 0)
    def _():
        zero = jnp.zeros((nb, nb), F32)
        for p in range(pp):
            s_scr[p] = jnp.concatenate([jnp.concatenate([s0_ref[2 * p], zero], 1),
                                        jnp.concatenate([zero, s0_ref[2 * p + 1]], 1)], 0)
        xr_scr[pad - 1:pad, :] = shr_ref[...]
        xk_scr[pad - 1:pad, :] = shk_ref[...]
        xv_scr[pad - 1:pad, :] = shv_ref[...]
        xl_scr[pad - 1:pad, :] = shl_ref[...]
        xg_scr[pad - 1:pad, :] = shg_ref[...]

    def shift_mix(x_ref, xs, mu_ref):
        x = x_ref[...].astype(F32)
        xs[pad:pad + tc, :] = x
        prev = xs[pad - 1:pad - 1 + tc, :]
        xs[pad - 1:pad, :] = xs[pad + tc - 1:pad + tc, :]
        return x + mu_ref[...] * (prev - x)

    xr = shift_mix(r_ref, xr_scr, mur_ref)
    xk = shift_mix(k_ref, xk_scr, muk_ref)
    xv = shift_mix(v_ref, xv_scr, muv_ref)
    xl = shift_mix(l_ref, xl_scr, mul_ref)
    xg = shift_mix(g_ref, xg_scr, mug_ref)

    lora_w = wup_ref.shape[0]
    w_log = -_softplus(-(w0_ref[...] + _nn(jnp.tanh(xl[:, :lora_w]), wup_ref[...]))) - 0.5
    logw_all = -jnp.exp(w_log)
    a_all = _sigmoid(a0_ref[...] + _nn(xl[:, lora_w:], aup_ref[...]))
    gate_all = _nn(_sigmoid(xg), gup_ref[...])

    lane = lax.broadcasted_iota(jnp.int32, (1, LANES), 1)
    m0 = lane < nb

    def per_head_sum(x):
        s_lo = jnp.sum(jnp.where(m0, x, 0.0), -1, keepdims=True)
        s_hi = jnp.sum(jnp.where(m0, 0.0, x), -1, keepdims=True)
        return jnp.where(m0, s_lo, s_hi)

    ra = lax.broadcasted_iota(jnp.int32, (3 * c_len, c2), 0)
    ca = lax.broadcasted_iota(jnp.int32, (3 * c_len, c2), 1)
    ca_t = jnp.where(ca >= c_len, ca - c_len, ca)
    ra_t = jnp.where(ra < c_len, ra - 1, (ra - c_len) % c_len)
    mask_a = ra_t >= ca_t
    rb_i = lax.broadcasted_iota(jnp.int32, (c_len, c2), 0)
    cb_i = lax.broadcasted_iota(jnp.int32, (c_len, c2), 1)
    first_cols = cb_i < c_len
    mask_b = rb_i > jnp.where(first_cols, cb_i, cb_i - c_len)
    ii = lax.broadcasted_iota(jnp.int32, (c2, c2), 0)
    jj = lax.broadcasted_iota(jnp.int32, (c2, c2), 1)
    eye2 = (ii == jj).astype(F32)
    si = lax.broadcasted_iota(jnp.int32, (LANES, LANES), 0)
    sj = lax.broadcasted_iota(jnp.int32, (LANES, LANES), 1)
    same_head = (si // nb) == (sj // nb)
    li = lax.broadcasted_iota(jnp.int32, (c_len, c_len), 0)
    lj = lax.broadcasted_iota(jnp.int32, (c_len, c_len), 1)
    ltri = (li >= lj).astype(BF16)
    n_sq = max(c_len.bit_length() - 2, 0)

    chains = [(p, c) for p in range(pp) for c in range(n_chunks)]
    r_c, k_c, b_c, a_c, v_c, lw_c, bonus_p = [], [], [], [], [], [], []
    for p in range(pp):
        ls = slice(p * LANES, (p + 1) * LANES)
        r_p, k_p, v_p, a_p = xr[:, ls], xk[:, ls], xv[:, ls], a_all[:, ls]
        kkv = k_p * kk_ref[:, ls]
        kk = kkv * lax.rsqrt(per_head_sum(kkv * kkv) + NORM_EPS)
        k2 = k_p * (1.0 + (a_p - 1.0) * ka_ref[:, ls])
        bv = kk * a_p
        bonus_p.append(per_head_sum(r_p * k2 * rk_ref[:, ls]) * v_p)
        for c in range(n_chunks):
            sl = slice(c * c_len, (c + 1) * c_len)
            r_c.append(r_p[sl]); k_c.append(k2[sl]); b_c.append(bv[sl]); a_c.append(-kk[sl])
            v_c.append(v_p[sl]); lw_c.append(logw_all[sl, ls])

    cl = [_cumsum_rows(ltri, lw) for lw in lw_c]
    cl_last = [x[c_len - 1:c_len, :] for x in cl]
    e_neg = [jnp.exp(-x) for x in cl]
    rt = [r * jnp.exp(x) for r, x in zip(r_c, cl)]
    kt = [k * e for k, e in zip(k_c, e_neg)]
    bt = [b * e for b, e in zip(b_c, e_neg)]
    at = [a * jnp.exp(x - lw) for a, x, lw in zip(a_c, cl, lw_c)]
    e_tail = [jnp.exp(xl - x) for xl, x in zip(cl_last, cl)]
    bk_bar = [jnp.concatenate([b * e, k * e], 0) for b, k, e in zip(b_c, k_c, e_tail)]
    w_last = [jnp.exp(x) for x in cl_last]
    at0 = [jnp.where(m0, x, 0.0) for x in at]
    at1 = [jnp.where(m0, 0.0, x) for x in at]
    g_a = [jnp.where(mask_a, _nt(jnp.concatenate([a0_, jnp.where(m0, r, 0.0), jnp.where(m0, 0.0, r)], 0),
                                 jnp.concatenate([b, k], 0)), 0.0)
           for a0_, r, b, k in zip(at0, rt, bt, kt)]
    g_b = [jnp.where(mask_b, _nt(a1_, jnp.concatenate([k, b], 0)), 0.0) for a1_, k, b in zip(at1, kt, bt)]
    p_pow = [jnp.concatenate([jnp.where(first_cols, ga[:c_len], 0.0), jnp.where(first_cols, 0.0, gb)], 0)
             for ga, gb in zip(g_a, g_b)]
    t_bd = [eye2 + x for x in p_pow]
    for _ in range(n_sq):
        p_pow = [_nn(x, x) for x in p_pow]
        t_bd = [t_ + _nn(t_, x) for t_, x in zip(t_bd, p_pow)]
    ak = [jnp.concatenate([jnp.where(first_cols, 0.0, ga[:c_len]), jnp.where(first_cols, gb, 0.0)], 0)
          for ga, gb in zip(g_a, g_b)]
    akv = [_nn(x, jnp.concatenate([jnp.where(m0, 0.0, v), jnp.where(m0, v, 0.0)], 0)) for x, v in zip(ak, v_c)]
    sol = [_nn(t_, jnp.concatenate([jnp.concatenate([a0_, a1_], 0), x], 1))
           for t_, a0_, a1_, x in zip(t_bd, at0, at1, akv)]
    wr = [jnp.concatenate([x[:c_len, :LANES] + x[c_len:, :LANES], r], 0) for x, r in zip(sol, rt)]
    u = [x[:c_len, LANES:] + x[c_len:, LANES:] for x in sol]

    s = [s_scr[p] for p in range(pp)]
    ys = [[] for _ in range(pp)]
    for c in range(n_chunks):
        idx = [chains.index((p, c)) for p in range(pp)]
        ws = [_nt(wr[i], s_) for i, s_ in zip(idx, s)]
        pv = [jnp.concatenate([u[i] + w_[:c_len], v_c[i]], 0) for i, w_ in zip(idx, ws)]
        yy = [_nn(g_a[i][c_len:], x) for i, x in zip(idx, pv)]
        for p, (w_, y_) in enumerate(zip(ws, yy)):
            ys[p].append(w_[c_len:] + jnp.where(m0, y_[:c_len], y_[c_len:]))
        s = [s_ * w_last[i] + jnp.where(same_head, _tn(x, bk_bar[i]), 0.0) for i, s_, x in zip(idx, s, pv)]

    for p in range(pp):
        ls = slice(p * LANES, (p + 1) * LANES)
        s_scr[p] = s[p]
        y = ys[p][0] if n_chunks == 1 else jnp.concatenate(ys[p], 0)
        mu = per_head_sum(y) * (1.0 / nb)
        yc = y - mu
        var = per_head_sum(yc * yc) * (1.0 / nb)
        y = yc * lax.rsqrt(var + GN_EPS) * lng_ref[:, ls] + lnb_ref[:, ls] + bonus_p[p]
        o_ref[:, ls] = (_sigmoid(gb_ref[:, ls].astype(F32)) * (y * gate_all[:, ls])).astype(o_ref.dtype)

    @pl.when(t == pl.num_programs(2) - 1)
    def _():
        for p in range(pp):
            s_p = s_scr[p]
            sout_ref[2 * p] = s_p[:nb, :nb]
            sout_ref[2 * p + 1] = s_p[nb:, nb:]


def _rwkv(proj, shift_state, s0, mu_b, w0, a0, k_k, k_a, r_k, lnx_g, lnx_b, w_up, a_up, g_up,
          cb, tc, chunk, pp):
    b, t_len, _ = proj.shape
    n_heads = s0.shape[1]
    n_pairs = n_heads * HEAD_B // LANES
    grp = n_pairs // pp
    n_t = t_len // tc
    wb = pp * LANES
    small = 3 * n_pairs

    def pcol(blk):
        return pl.BlockSpec((None, tc, wb), lambda bi, pi, ti: (bi, ti, pi + blk // pp))

    def pfix(blk):
        return pl.BlockSpec((None, tc, LANES), lambda bi, pi, ti: (bi, ti, blk))

    def scol(g):
        return pl.BlockSpec((None, 1, wb), lambda bi, pi, ti: (bi, 0, pi + g * grp))

    def sfix(blk):
        return pl.BlockSpec((None, 1, LANES), lambda bi, pi, ti: (bi, 0, blk))

    def mcol(g):
        return pl.BlockSpec((1, wb), lambda bi, pi, ti: (0, pi + g * grp))

    def mfix(blk):
        return pl.BlockSpec((1, LANES), lambda bi, pi, ti: (0, blk))

    prow = pl.BlockSpec((1, wb), lambda bi, pi, ti: (0, pi))
    state = pl.BlockSpec((None, 2 * pp, HEAD_B, HEAD_B), lambda bi, pi, ti: (bi, pi, 0, 0))
    kern = functools.partial(_rwkv_kernel, chunk=chunk, n_chunks=tc // chunk, pp=pp)
    return pl.pallas_call(
        kern,
        grid=(b, grp, n_t),
        in_specs=[pcol(cb["br"]), pcol(cb["bk"]), pcol(cb["bv"]), pcol(cb["gate_b"]),
                  pfix(cb["lora"]), pfix(cb["gd"]),
                  scol(0), scol(1), scol(2), sfix(small), sfix(small + 1),
                  mcol(0), mcol(1), mcol(2), mfix(small), mfix(small + 1),
                  prow, prow, prow, prow, prow, prow, prow,
                  pl.BlockSpec((w_up.shape[0], wb), lambda bi, pi, ti: (0, pi)),
                  pl.BlockSpec((a_up.shape[0], wb), lambda bi, pi, ti: (0, pi)),
                  pl.BlockSpec((g_up.shape[0], wb), lambda bi, pi, ti: (0, pi)),
                  state],
        out_specs=[pl.BlockSpec((None, tc, wb), lambda bi, pi, ti: (bi, ti, pi)), state],
        out_shape=[jax.ShapeDtypeStruct((b, t_len, n_heads * HEAD_B), BF16),
                   jax.ShapeDtypeStruct(s0.shape, F32)],
        scratch_shapes=[pltpu.VMEM((pp, LANES, LANES), F32)] + [pltpu.VMEM((tc + 8, wb), F32)] * 3
                       + [pltpu.VMEM((tc + 8, LANES), F32)] * 2,
        compiler_params=pltpu.CompilerParams(
            dimension_semantics=("parallel", "parallel", "arbitrary"), vmem_limit_bytes=VMEM_LIMIT),
        name="rwkv",
    )(proj, proj, proj, proj, proj, proj,
      shift_state, shift_state, shift_state, shift_state, shift_state,
      mu_b, mu_b, mu_b, mu_b, mu_b,
      w0, a0, k_k, k_a, r_k, lnx_g, lnx_b, w_up, a_up, g_up, s0)


def _mix_ffn_kernel(x_ref, ma_ref, mb_ref, wo_ref, g1_ref, b1_ref, w1_ref, w2_ref, g2_ref, b2_ref,
                    y_ref, *, alpha, ff_chunk):
    merged = ma_ref[...] + mb_ref[...]
    mix = jnp.dot(merged, wo_ref[...], preferred_element_type=F32)
    h = _layer_norm(alpha * x_ref[...] + mix, g1_ref[...], b1_ref[...])
    hb = h.astype(BF16)
    d_ff = w1_ref.shape[1]
    ff = jnp.zeros_like(h)
    for c in range(d_ff // ff_chunk):
        cs = slice(c * ff_chunk, (c + 1) * ff_chunk)
        a = jnp.maximum(jnp.dot(hb, w1_ref[:, cs], preferred_element_type=F32), 0.0)
        ff = ff + jnp.dot((a * a).astype(BF16), w2_ref[cs, :], preferred_element_type=F32)
    y_ref[...] = _layer_norm(alpha * h + ff, g2_ref[...], b2_ref[...])


def _mix_ffn(x2d, ma, mb, w_out, ln1_g, ln1_b, w_ff1, w_ff2, ln2_g, ln2_b, alpha, tm):
    n, d = x2d.shape
    d_ff = w_ff1.shape[1]

    def const(shape):
        return pl.BlockSpec(shape, lambda i: (0, 0), pipeline_mode=pl.Buffered(1))

    tile = pl.BlockSpec((tm, d), lambda i: (i, 0))
    kern = functools.partial(_mix_ffn_kernel, alpha=alpha, ff_chunk=min(d_ff, 1024))
    return pl.pallas_call(
        kern,
        grid=(n // tm,),
        in_specs=[tile, tile, tile, const((d, d)), const((1, d)), const((1, d)),
                  const((d, d_ff)), const((d_ff, d)), const((1, d)), const((1, d))],
        out_specs=tile,
        out_shape=jax.ShapeDtypeStruct((n, d), F32),
        compiler_params=pltpu.CompilerParams(
            dimension_semantics=("parallel",), vmem_limit_bytes=VMEM_LIMIT),
        name="mix_ffn",
    )(x2d, ma, mb, w_out, ln1_g, ln1_b, w_ff1, w_ff2, ln2_g, ln2_b)


def _pick_tile(n, target):
    t = min(n, target)
    while n % t:
        t //= 2
    return t


def _layer(x, conv_buf, s_gdn, shift_buf, s_wkv, p):
    b, t_len, d = x.shape
    n = b * t_len
    cb = p["cb"]
    x2d = x.reshape(n, d)

    proj = _proj(x2d, p["w_in"], _pick_tile(n, 1024), p["proj_tn"]).reshape(b, t_len, -1)

    chunk = min(CHUNK, t_len)
    tc = _pick_tile(t_len, TIME_TILE)
    ma, s_gdn_new = _gdn(proj, conv_buf, s_gdn, p["conv_w"], p["alog_row"], p["dtb_row"],
                         p["gdn_norm_g"], cb, tc, chunk, GDN_HEADS_PER_STEP)
    mb, s_wkv_new = _rwkv(proj, shift_buf[:, None, :], s_wkv, p["mu_b"], p["w0"], p["a0"], p["k_k"],
                          p["k_a"], p["r_k"], p["lnx_g"], p["lnx_b"], p["w_up"], p["a_up"], p["g_up"],
                          cb, tc, chunk, RWKV_PAIRS_PER_STEP)

    y = _mix_ffn(x2d, ma.reshape(n, -1), mb.reshape(n, -1), p["w_out"], p["ln1_g"], p["ln1_b"],
                 p["w_ff1"], p["w_ff2"], p["ln2_g"], p["ln2_b"], p["alpha"], _pick_tile(n, 512))

    conv_cols = conv_buf.shape[-1]
    hist = CONV_W - 1
    assert t_len >= hist
    conv_new = proj[:, t_len - hist:, :conv_cols].astype(x.dtype)
    last = proj[:, t_len - 1]
    shift_new = jnp.concatenate([last[:, cb["br"] * LANES:cb["gate_a"] * LANES],
                                 last[:, cb["lora"] * LANES:cb["ba"] * LANES]], -1).astype(x.dtype)
    return y.reshape(b, t_len, d), conv_new, s_gdn_new, shift_new, s_wkv_new


def _prep_layer(l, depth, w_in, conv_a_w, a_log, dt_bias, gdn_norm_g, mu_b, w0, w_up, a0, a_up, g_up,
                k_k, k_a, r_k, lnx_g, lnx_b, w_out, ln1_g, ln1_b, w_ff1, w_ff2, ln2_g, ln2_b):
    n_heads_a = a_log.shape[1]
    conv_ch = conv_a_w.shape[2]
    d_model = w_out.shape[1]
    v_a = n_heads_a * HEAD_A
    cols_b = mu_b.shape[1]
    small0 = conv_ch + v_a
    small1 = small0 + 2 * n_heads_a
    big_b = cols_b - (w_up.shape[1] + a_up.shape[1] + g_up.shape[1])
    w = w_in[l]
    w_p = jnp.concatenate(
        [w[:, :small0], w[:, small1:small1 + big_b], w[:, small1 + cols_b:], w[:, small1 + big_b:small1 + cols_b],
         jnp.pad(w[:, small0:small1], ((0, 0), (0, LANES - 2 * n_heads_a)))], axis=1).astype(BF16)
    blk = lambda c: c // LANES
    cb = {"q": 0, "k": blk(n_heads_a * HEAD_A), "v": blk(2 * n_heads_a * HEAD_A), "z": blk(conv_ch),
          "br": blk(small0)}
    cb["bk"] = cb["br"] + blk(big_b) // 3
    cb["bv"] = cb["bk"] + blk(big_b) // 3
    cb["gate_a"] = cb["br"] + blk(big_b)
    cb["gate_b"] = cb["gate_a"] + blk(d_model)
    cb["lora"] = cb["gate_b"] + blk(d_model)
    cb["gd"] = cb["lora"] + blk(w_up.shape[1] + a_up.shape[1])
    cb["ba"] = cb["gd"] + blk(g_up.shape[1])
    n_blocks = w_p.shape[1] // LANES
    tn_blocks = max(f for f in range(1, 17) if n_blocks % f == 0)

    def lane_row(vec, offset):
        return jnp.zeros((1, LANES), F32).at[0, offset:offset + vec.shape[0]].set(vec)

    row = lambda a: a[l].reshape(1, -1)
    return {
        "cb": cb, "proj_tn": tn_blocks * LANES, "w_in": w_p, "conv_w": conv_a_w[l],
        "alog_row": lane_row(a_log[l], n_heads_a), "dtb_row": lane_row(dt_bias[l], n_heads_a),
        "gdn_norm_g": row(gdn_norm_g), "mu_b": row(mu_b), "w0": row(w0), "a0": row(a0),
        "k_k": row(k_k), "k_a": row(k_a), "r_k": row(r_k), "lnx_g": row(lnx_g), "lnx_b": row(lnx_b),
        "w_up": w_up[l].astype(BF16), "a_up": a_up[l].astype(BF16), "g_up": g_up[l].astype(BF16),
        "w_out": w_out[l].astype(BF16), "ln1_g": row(ln1_g), "ln1_b": row(ln1_b),
        "w_ff1": w_ff1[l].astype(BF16), "w_ff2": w_ff2[l].astype(BF16),
        "ln2_g": row(ln2_g), "ln2_b": row(ln2_b), "alpha": float((2 * depth) ** 0.25),
    }


def kernel(x_prompt, x_sample, state_conv_a, state_gdn, state_shift_b, state_wkv, w_in, conv_a_w, a_log, dt_bias, gdn_norm_g, mu_b, w0, w_up, a0, a_up, g_up, k_k, k_a, r_k, lnx_g, lnx_b, w_out, ln1_g, ln1_b, w_ff1, w_ff2, ln2_g, ln2_b):
    weights = (w_in, conv_a_w, a_log, dt_bias, gdn_norm_g, mu_b, w0, w_up, a0, a_up, g_up,
               k_k, k_a, r_k, lnx_g, lnx_b, w_out, ln1_g, ln1_b, w_ff1, w_ff2, ln2_g, ln2_b)
    depth = w_in.shape[0]
    bp, dtype = x_prompt.shape[0], x_prompt.dtype
    hp, hs = x_prompt, x_sample
    out_p, out_s = [], []
    for l in range(depth):
        p = _prep_layer(l, depth, *weights)
        zeros = lambda ref: jnp.zeros((bp,) + ref.shape[2:], dtype)
        hp, *st_p = _layer(hp, zeros(state_conv_a), zeros(state_gdn), zeros(state_shift_b),
                           zeros(state_wkv), p)
        hs, *st_s = _layer(hs, state_conv_a[l], state_gdn[l], state_shift_b[l], state_wkv[l], p)
        out_p.append(st_p)
        out_s.append(st_s)
    conv_p, gdn_p, shift_p, wkv_p = (jnp.stack([s[i] for s in out_p]) for i in range(4))
    conv_s, gdn_s, shift_s, wkv_s = (jnp.stack([s[i] for s in out_s]) for i in range(4))
    return (hp, hs, conv_p, gdn_p, shift_p, wkv_p, conv_s, gdn_s, shift_s, wkv_s)
```

```python
import functools

import jax
import jax.numpy as jnp
from jax import lax
from jax.experimental import pallas as pl
from jax.experimental.pallas import tpu as pltpu

F32 = jnp.float32
BF16 = jnp.bfloat16

LANES = 128
MXU_COLS = 256
SUB = 8
BF16_ROWS = 16
HEAD_A = 128
HEAD_B = 64
CONV_W = 4
LN_EPS = 1e-5
NORM_EPS = 1e-6
GN_EPS = 64e-5
CHUNK = 64
TIME_TILE = 256
VMEM_LIMIT = 56 * 1024 * 1024


def _dot(a, b, dims):
    return lax.dot_general(a.astype(BF16), b.astype(BF16), (dims, ((), ())), preferred_element_type=F32)


def _nn(a, b):
    return _dot(a, b, ((1,), (0,)))


def _nt(a, b):
    return _dot(a, b, ((1,), (1,)))


def _tn(a, b):
    return _dot(a, b, ((0,), (0,)))


def _sigmoid(x):
    return 0.5 * jnp.tanh(0.5 * x) + 0.5


def _silu(x):
    h = 0.5 * x
    return h + h * jnp.tanh(h)


def _softplus(x):
    return jnp.maximum(x, 0.0) + jnp.log1p(jnp.exp(-jnp.abs(x)))


def _layer_norm(x, g, b):
    mu = jnp.mean(x, -1, keepdims=True)
    xc = x - mu
    var = jnp.mean(xc * xc, -1, keepdims=True)
    return xc * lax.rsqrt(var + LN_EPS) * g + b


def _tri_masks(c):
    ii = lax.broadcasted_iota(jnp.int32, (c, c), 0)
    jj = lax.broadcasted_iota(jnp.int32, (c, c), 1)
    return ii >= jj, ii > jj, ii == jj, ii <= jj


def _cumsum_rows(ltri, x):
    n = x.shape[1]
    hi = x.astype(BF16)
    r1 = x - hi.astype(F32)
    mid = r1.astype(BF16)
    lo = (r1 - mid.astype(F32)).astype(BF16)
    res = jnp.dot(ltri, jnp.concatenate([hi, mid, lo], -1), preferred_element_type=F32)
    return res[:, :n] + res[:, n:2 * n] + res[:, 2 * n:]


def _interleave(a, b):
    return [x for pair in zip(a, b) for x in pair]


def _proj_kernel(x_ref, w_ref, o_ref, xb_ref):
    @pl.when(pl.program_id(1) == 0)
    def _():
        xb_ref[...] = x_ref[...].astype(BF16)

    o_ref[...] = jnp.dot(xb_ref[...], w_ref[...], preferred_element_type=F32).astype(o_ref.dtype)


def _proj(x2d, w_p, tm, tn):
    n, d = x2d.shape
    cols = w_p.shape[1]
    return pl.pallas_call(
        _proj_kernel,
        grid=(n // tm, cols // tn),
        in_specs=[pl.BlockSpec((tm, d), lambda i, j: (i, 0)),
                  pl.BlockSpec((d, tn), lambda i, j: (0, j))],
        out_specs=pl.BlockSpec((tm, tn), lambda i, j: (i, j)),
        out_shape=jax.ShapeDtypeStruct((n, cols), BF16),
        scratch_shapes=[pltpu.VMEM((tm, d), BF16)],
        compiler_params=pltpu.CompilerParams(
            dimension_semantics=("parallel", "arbitrary"), vmem_limit_bytes=VMEM_LIMIT),
        name="proj",
    )(x2d, w_p)


def _mixer_kernel(
        q_ref, k_ref, v_ref, z_ref, ga_ref, ba_ref, cq_ref, ck_ref, cv_ref, sg0_ref,
        wq_ref, wk_ref, wv_ref, alog_ref, dtb_ref, ng_ref,
        r_ref, bk_ref, bv_ref, gb_ref, l_ref, g_ref,
        shr_ref, shk_ref, shv_ref, shl_ref, shg_ref,
        mur_ref, muk_ref, muv_ref, mul_ref, mug_ref,
        w0_ref, a0_ref, kk_ref, ka_ref, rk_ref, lng_ref, lnb_ref,
        wup_ref, aup_ref, gup_ref, sw0_ref,
        o_ref, sg_out_ref, sw_out_ref,
        sg_scr, sw_scr, xq_scr, xk_scr, xv_scr, xr_scr, xbk_scr, xbv_scr, xl_scr, xg_scr,
        *, chunk, n_chunks):
    t = pl.program_id(1)
    c_len = chunk
    c2 = 2 * chunk
    tc = chunk * n_chunks
    hist = CONV_W - 1
    nb = HEAD_B
    n_heads = sg_scr.shape[0]
    n_pairs = sw_scr.shape[0]
    per_group = MXU_COLS // LANES
    group_w = per_group * LANES

    @pl.when(t == 0)
    def _():
        sg_scr[...] = sg0_ref[...]
        xq_scr[SUB - hist:SUB, :] = cq_ref[...]
        xk_scr[SUB - hist:SUB, :] = ck_ref[...]
        xv_scr[SUB - hist:SUB, :] = cv_ref[...]
        zero = jnp.zeros((nb, nb), F32)
        for p in range(n_pairs):
            sw_scr[p] = jnp.concatenate([jnp.concatenate([sw0_ref[2 * p], zero], 1),
                                         jnp.concatenate([zero, sw0_ref[2 * p + 1]], 1)], 0)
        xr_scr[SUB - 1:SUB, :] = shr_ref[...]
        xbk_scr[SUB - 1:SUB, :] = shk_ref[...]
        xbv_scr[SUB - 1:SUB, :] = shv_ref[...]
        xl_scr[SUB - 1:SUB, :] = shl_ref[...]
        xg_scr[SUB - 1:SUB, :] = shg_ref[...]

    def shift_matrix(n_shifts, k_rows, offset):
        rr = lax.broadcasted_iota(jnp.int32, (n_shifts * c_len, k_rows), 0)
        cc = lax.broadcasted_iota(jnp.int32, (n_shifts * c_len, k_rows), 1)
        return (cc == rr % c_len - rr // c_len - 1 + offset).astype(BF16)

    def shifted(x_ref, hs, c, n_shifts, cols):
        if c == 0:
            sh = jnp.dot(shift_matrix(n_shifts, c_len, 0), x_ref[0:c_len, cols], preferred_element_type=F32)
            row = lax.broadcasted_iota(jnp.int32, (SUB, 1), 0)
            h_rows = hs[:, cols]
            out = []
            for j in range(n_shifts):
                blk = sh[j * c_len:(j + 1) * c_len]
                fix = jnp.where(row < j + 1, pltpu.roll(h_rows, j + 1, 0), 0.0)
                out.append(jnp.concatenate([blk[:SUB] + fix, blk[SUB:]], 0) if c_len > SUB else blk + fix)
            return out
        lo = c * c_len - BF16_ROWS
        sh = jnp.dot(shift_matrix(n_shifts, c_len + BF16_ROWS, BF16_ROWS), x_ref[lo:lo + c_len + BF16_ROWS, cols],
                     preferred_element_type=F32)
        return [sh[j * c_len:(j + 1) * c_len] for j in range(n_shifts)]

    lane = lax.broadcasted_iota(jnp.int32, (1, LANES), 1)
    m0 = lane < nb
    lora_w = wup_ref.shape[0]

    def per_head_sum(x):
        s_lo = jnp.sum(jnp.where(m0, x, 0.0), -1, keepdims=True)
        s_hi = jnp.sum(jnp.where(m0, 0.0, x), -1, keepdims=True)
        return jnp.where(m0, s_lo, s_hi)

    incl, strict, eye_m, upper = _tri_masks(c_len)
    eye = eye_m.astype(F32)
    ltri = incl.astype(BF16)
    n_sq = max(c_len.bit_length() - 2, 0)
    ra = lax.broadcasted_iota(jnp.int32, (3 * c_len, c2), 0)
    ca = lax.broadcasted_iota(jnp.int32, (3 * c_len, c2), 1)
    ca_t = jnp.where(ca >= c_len, ca - c_len, ca)
    ra_t = jnp.where(ra < c_len, ra - 1, (ra - c_len) % c_len)
    mask_a = ra_t >= ca_t
    rb_i = lax.broadcasted_iota(jnp.int32, (c_len, c2), 0)
    cb_i = lax.broadcasted_iota(jnp.int32, (c_len, c2), 1)
    first_cols = cb_i < c_len
    mask_b = rb_i > jnp.where(first_cols, cb_i, cb_i - c_len)
    ii2 = lax.broadcasted_iota(jnp.int32, (c2, c2), 0)
    jj2 = lax.broadcasted_iota(jnp.int32, (c2, c2), 1)
    eye2 = (ii2 == jj2).astype(F32)
    si = lax.broadcasted_iota(jnp.int32, (LANES, LANES), 0)
    sj = lax.broadcasted_iota(jnp.int32, (LANES, LANES), 1)
    same_head = (si // nb) == (sj // nb)

    st = [dict() for _ in range(n_chunks)]
    sg = [sg_scr[h] for h in range(n_heads)]
    sw = [sw_scr[p] for p in range(n_pairs)]

    def front(c):
        d = st[c]
        rows = slice(c * c_len, (c + 1) * c_len)

        def conv_step(name, x_ref, hs, w_ref, grp):
            def run():
                cols = slice(grp * group_w, (grp + 1) * group_w)
                w = w_ref[:, cols]
                acc = x_ref[rows, cols].astype(F32) * w[hist:CONV_W, :]
                for j, prev in enumerate(shifted(x_ref, hs, c, hist, cols)):
                    acc = acc + prev * w[hist - 1 - j:hist - j, :]
                d[name, grp] = _silu(acc)
            return run

        def mix_step(name, x_ref, hs, mu_ref, grp):
            def run():
                cols = slice(grp * group_w, (grp + 1) * group_w) if x_ref.shape[1] > group_w else slice(None)
                x = x_ref[rows, cols].astype(F32)
                d[name, grp] = x + mu_ref[:, cols] * (shifted(x_ref, hs, c, 1, cols)[0] - x)
            return run

        def wide(name, h):
            off = h % per_group * LANES
            return d[name, h // per_group][:, off:off + LANES]

        def gates():
            ba = ba_ref[rows, :].astype(F32)
            d["beta_all"] = _sigmoid(ba)
            d["g_all"] = -jnp.exp(alog_ref[...]) * _softplus(ba + dtb_ref[...])
            xl = d["xl", 0]
            w_log = -_softplus(-(w0_ref[...] + _nn(jnp.tanh(xl[:, :lora_w]), wup_ref[...]))) - 0.5
            d["logw"] = -jnp.exp(w_log)
            d["a_all"] = _sigmoid(a0_ref[...] + _nn(xl[:, lora_w:], aup_ref[...]))
            d["gate"] = _nn(_sigmoid(d["xg", 0]), gup_ref[...])

        def gdn_prep(h):
            def run():
                ls = slice(h * HEAD_A, (h + 1) * HEAD_A)
                qh, kh, v = wide("qc", h), wide("kc", h), wide("vc", h)
                q = qh * lax.rsqrt(jnp.sum(qh * qh, -1, keepdims=True) + NORM_EPS) * (HEAD_A ** -0.5)
                k = kh * lax.rsqrt(jnp.sum(kh * kh, -1, keepdims=True) + NORM_EPS)
                beta = jnp.sum(jnp.where(lane == h, d["beta_all"], 0.0), -1, keepdims=True)
                g = jnp.sum(jnp.where(lane == h + n_heads, d["g_all"], 0.0), -1, keepdims=True)
                g_row = jnp.sum(jnp.where(eye_m, g, 0.0), 0, keepdims=True)
                gc = jnp.sum(jnp.where(incl, g_row, 0.0), 1, keepdims=True)
                gc_row = jnp.sum(jnp.where(upper, g, 0.0), 0, keepdims=True)
                eg = jnp.exp(gc)
                g_last = gc[c_len - 1:c_len, :]
                kb = k * beta
                d["g", h] = {
                    "decay": jnp.where(incl, jnp.exp(jnp.where(incl, gc - gc_row, 0.0)), 0.0),
                    "k": k, "kbq": jnp.concatenate([kb, q], 0),
                    "rhs": jnp.concatenate([v * beta, kb * eg], -1), "qe": q * eg,
                    "kd": k * jnp.exp(g_last - gc), "e_last": jnp.exp(g_last)}
            return run

        def rwkv_prep(p):
            def run():
                ls = slice(p * LANES, (p + 1) * LANES)
                r_p, k_p, v_p, a_p = wide("xr", p), wide("xk", p), wide("xv", p), d["a_all"][:, ls]
                kkv = k_p * kk_ref[:, ls]
                kk = kkv * lax.rsqrt(per_head_sum(kkv * kkv) + NORM_EPS)
                k2 = k_p * (1.0 + (a_p - 1.0) * ka_ref[:, ls])
                bvec = kk * a_p
                lw = d["logw"][:, ls]
                cl = _cumsum_rows(ltri, lw)
                cl_last = cl[c_len - 1:c_len, :]
                e_neg = jnp.exp(-cl)
                rt = r_p * jnp.exp(cl)
                kt = k2 * e_neg
                bt = bvec * e_neg
                at = -kk * jnp.exp(cl - lw)
                e_tail = jnp.exp(cl_last - cl)
                at0 = jnp.where(m0, at, 0.0)
                at1 = jnp.where(m0, 0.0, at)
                d["r", p] = {
                    "bonus": per_head_sum(r_p * k2 * rk_ref[:, ls]) * v_p, "v": v_p, "rt": rt,
                    "lhs_a": jnp.concatenate([at0, jnp.where(m0, rt, 0.0), jnp.where(m0, 0.0, rt)], 0),
                    "rhs_a": jnp.concatenate([bt, kt], 0), "at1": at1, "rhs_b": jnp.concatenate([kt, bt], 0),
                    "a01": jnp.concatenate([at0, at1], 0),
                    "v10": jnp.concatenate([jnp.where(m0, 0.0, v_p), jnp.where(m0, v_p, 0.0)], 0),
                    "bk_bar": jnp.concatenate([bvec * e_tail, k2 * e_tail], 0), "w_last": jnp.exp(cl_last)}
            return run

        steps = [(1, mix_step("xl", l_ref, xl_scr, mul_ref, 0)), (1, mix_step("xg", g_ref, xg_scr, mug_ref, 0)),
                 (8, gates)]
        for grp in range(n_heads // per_group):
            steps += [(2, conv_step("qc", q_ref, xq_scr, wq_ref, grp)), (2, conv_step("kc", k_ref, xk_scr, wk_ref, grp)),
                      (2, conv_step("vc", v_ref, xv_scr, wv_ref, grp)),
                      (1, mix_step("xr", r_ref, xr_scr, mur_ref, grp)), (1, mix_step("xk", bk_ref, xbk_scr, muk_ref, grp)),
                      (1, mix_step("xv", bv_ref, xbv_scr, muv_ref, grp))]
            for h in range(grp * per_group, (grp + 1) * per_group):
                steps += [(3, gdn_prep(h)), (4, rwkv_prep(h))]
        return steps

    def mxu(c):
        d = st[c]

        def gram(i):
            def run():
                a, b = d["g", i], d["r", i]
                gm = _nt(a["kbq"], a["k"])
                a["qk"] = gm[c_len:] * a["decay"]
                a["p"] = jnp.where(strict, -gm[:c_len] * a["decay"], 0.0)
                a["t"] = eye + a["p"]
                g_a = jnp.where(mask_a, _nt(b["lhs_a"], b["rhs_a"]), 0.0)
                g_b = jnp.where(mask_b, _nt(b["at1"], b["rhs_b"]), 0.0)
                top = g_a[:c_len]
                b["rbk"] = g_a[c_len:]
                b["p"] = jnp.concatenate([jnp.where(first_cols, top, 0.0), jnp.where(first_cols, 0.0, g_b)], 0)
                b["t"] = eye2 + b["p"]
                b["ak"] = jnp.concatenate([jnp.where(first_cols, 0.0, top), jnp.where(first_cols, g_b, 0.0)], 0)
            return run

        def square(i):
            def run():
                for m in (d["g", i], d["r", i]):
                    m["p"] = _nn(m["p"], m["p"])
            return run

        def extend(i):
            def run():
                for m in (d["g", i], d["r", i]):
                    m["t"] = m["t"] + _nn(m["t"], m["p"])
            return run

        def wy(i):
            def run():
                a, b = d["g", i], d["r", i]
                sol = _nn(a["t"], a["rhs"])
                a["u"] = sol[:, :HEAD_A]
                a["wq"] = jnp.concatenate([sol[:, HEAD_A:], a["qe"]], 0)
                akv = _nn(b["ak"], b["v10"])
                sol_r = _nn(b["t"], jnp.concatenate([b["a01"], akv], 1))
                b["wr"] = jnp.concatenate([sol_r[:c_len, :LANES] + sol_r[c_len:, :LANES], b["rt"]], 0)
                b["u"] = sol_r[:c_len, LANES:] + sol_r[c_len:, LANES:]
            return run

        def read_state(i):
            def run():
                a, b = d["g", i], d["r", i]
                a["ws"] = _nn(a["wq"], sg[i])
                b["ws"] = _nt(b["wr"], sw[i])
            return run

        def update_state(i):
            def run():
                a, b = d["g", i], d["r", i]
                a["v_new"] = a["u"] - a["ws"][:c_len]
                b["pv"] = jnp.concatenate([b["u"] + b["ws"][:c_len], b["v"]], 0)
                sg[i] = sg[i] * a["e_last"] + _tn(a["kd"], a["v_new"])
                sw[i] = sw[i] * b["w_last"] + jnp.where(same_head, _tn(b["pv"], b["bk_bar"]), 0.0)
            return run

        def outputs(i):
            def run():
                a, b = d["g", i], d["r", i]
                a["o"] = a["ws"][c_len:] + _nn(a["qk"], a["v_new"])
                y_ = _nn(b["rbk"], b["pv"])
                b["y"] = b["ws"][c_len:] + jnp.where(m0, y_[:c_len], y_[c_len:])
            return run

        heads = range(n_heads)
        steps = [(6, gram(i)) for i in heads]
        for _ in range(n_sq):
            steps += [(3, square(i)) for i in heads] + [(3, extend(i)) for i in heads]
        steps += [(5, wy(i)) for i in heads] + [(4, read_state(i)) for i in heads]
        return steps + [(4, update_state(i)) for i in heads] + [(3, outputs(i)) for i in heads]

    def back(c):
        d = st[c]
        rows = slice(c * c_len, (c + 1) * c_len)

        def head(h):
            def run():
                ls = slice(h * LANES, (h + 1) * LANES)
                o = d["g", h]["o"]
                o = o * lax.rsqrt(jnp.mean(o * o, -1, keepdims=True) + NORM_EPS) * ng_ref[...]
                z = z_ref[rows, ls].astype(F32)
                o_a = o * _silu(z)
                y = d["r", h]["y"]
                mu = per_head_sum(y) * (1.0 / nb)
                yc = y - mu
                var = per_head_sum(yc * yc) * (1.0 / nb)
                y = yc * lax.rsqrt(var + GN_EPS) * lng_ref[:, ls] + lnb_ref[:, ls] + d["r", h]["bonus"]
                o_b = y * d["gate"][:, ls]
                o_ref[rows, ls] = (_sigmoid(ga_ref[rows, ls].astype(F32)) * o_a
                                   + _sigmoid(gb_ref[rows, ls].astype(F32)) * o_b).astype(o_ref.dtype)
            return run

        return [(1, head(h)) for h in range(n_heads)]

    def run_merged(*lists):
        order = []
        for k, steps in enumerate(lists):
            total, done = sum(w for w, _ in steps), 0.0
            for i, (w, _) in enumerate(steps):
                order.append(((done + 0.5 * w) / total, k, i))
                done += w
        for _, k, i in sorted(order):
            lists[k][i][1]()

    run_merged(front(0))
    for c in range(n_chunks):
        stages = [mxu(c)]
        if c + 1 < n_chunks:
            stages.append(front(c + 1))
        if c > 0:
            stages.append(back(c - 1))
        run_merged(*stages)
    run_merged(back(n_chunks - 1))

    for h in range(n_heads):
        sg_scr[h] = sg[h]
        sw_scr[h] = sw[h]
    for x_ref, hs in ((q_ref, xq_scr), (k_ref, xk_scr), (v_ref, xv_scr), (r_ref, xr_scr), (bk_ref, xbk_scr),
                      (bv_ref, xbv_scr), (l_ref, xl_scr), (g_ref, xg_scr)):
        hs[...] = x_ref[tc - SUB:tc, :].astype(F32)

    @pl.when(t == pl.num_programs(1) - 1)
    def _():
        sg_out_ref[...] = sg_scr[...]
        for p in range(n_pairs):
            s_p = sw_scr[p]
            sw_out_ref[2 * p] = s_p[:nb, :nb]
            sw_out_ref[2 * p + 1] = s_p[nb:, nb:]


def _mixers(proj, conv_state, s_gdn, shift_state, s_wkv, p, tc, chunk):
    b, t_len, _ = proj.shape
    cb = p["cb"]
    n_heads = s_gdn.shape[1]
    d = n_heads * HEAD_A
    n_pairs = s_wkv.shape[1] * HEAD_B // LANES
    assert n_pairs == n_heads and d == n_pairs * LANES
    n_t = t_len // tc
    per = d // LANES
    small = 3 * per

    def pcol(blk):
        return pl.BlockSpec((None, tc, d), lambda bi, ti: (bi, ti, blk // per))

    def pfix(blk):
        return pl.BlockSpec((None, tc, LANES), lambda bi, ti: (bi, ti, blk))

    def ccol(g):
        return pl.BlockSpec((None, CONV_W - 1, d), lambda bi, ti: (bi, 0, g))

    def wcol(g):
        return pl.BlockSpec((CONV_W, d), lambda bi, ti: (0, g))

    def scol(g):
        return pl.BlockSpec((None, 1, d), lambda bi, ti: (bi, 0, g))

    def sfix(blk):
        return pl.BlockSpec((None, 1, LANES), lambda bi, ti: (bi, 0, blk))

    def mcol(g):
        return pl.BlockSpec((1, d), lambda bi, ti: (0, g))

    def mfix(blk):
        return pl.BlockSpec((1, LANES), lambda bi, ti: (0, blk))

    def full(a):
        return pl.BlockSpec(a.shape, lambda bi, ti: (0,) * a.ndim)

    row = pl.BlockSpec((1, LANES), lambda bi, ti: (0, 0))
    st_g = pl.BlockSpec((None,) + s_gdn.shape[1:], lambda bi, ti: (bi, 0, 0, 0))
    st_w = pl.BlockSpec((None,) + s_wkv.shape[1:], lambda bi, ti: (bi, 0, 0, 0))
    prows = [p[n] for n in ("w0", "a0", "k_k", "k_a", "r_k", "lnx_g", "lnx_b", "w_up", "a_up", "g_up")]
    kern = functools.partial(_mixer_kernel, chunk=chunk, n_chunks=tc // chunk)
    wide = pltpu.VMEM((SUB, d), F32)
    narrow = pltpu.VMEM((SUB, LANES), F32)
    return pl.pallas_call(
        kern,
        grid=(b, n_t),
        in_specs=[pcol(cb["q"]), pcol(cb["k"]), pcol(cb["v"]), pcol(cb["z"]), pcol(cb["gate_a"]), pfix(cb["ba"]),
                  ccol(0), ccol(1), ccol(2), st_g, wcol(0), wcol(1), wcol(2), row, row, row,
                  pcol(cb["br"]), pcol(cb["bk"]), pcol(cb["bv"]), pcol(cb["gate_b"]),
                  pfix(cb["lora"]), pfix(cb["gd"]),
                  scol(0), scol(1), scol(2), sfix(small), sfix(small + 1),
                  mcol(0), mcol(1), mcol(2), mfix(small), mfix(small + 1)]
                 + [full(a) for a in prows] + [st_w],
        out_specs=[pl.BlockSpec((None, tc, d), lambda bi, ti: (bi, ti, 0)), st_g, st_w],
        out_shape=[jax.ShapeDtypeStruct((b, t_len, d), BF16),
                   jax.ShapeDtypeStruct(s_gdn.shape, F32), jax.ShapeDtypeStruct(s_wkv.shape, F32)],
        scratch_shapes=[pltpu.VMEM((n_heads, HEAD_A, HEAD_A), F32), pltpu.VMEM((n_pairs, LANES, LANES), F32),
                        wide, wide, wide, wide, wide, wide, narrow, narrow],
        compiler_params=pltpu.CompilerParams(
            dimension_semantics=("parallel", "arbitrary"), vmem_limit_bytes=VMEM_LIMIT),
        name="mixers",
    )(proj, proj, proj, proj, proj, proj, conv_state, conv_state, conv_state, s_gdn,
      p["conv_w"], p["conv_w"], p["conv_w"], p["alog_row"], p["dtb_row"], p["gdn_norm_g"],
      proj, proj, proj, proj, proj, proj,
      shift_state, shift_state, shift_state, shift_state, shift_state,
      p["mu_b"], p["mu_b"], p["mu_b"], p["mu_b"], p["mu_b"], *prows, s_wkv)


def _mix_ffn_kernel(x_ref, m_ref, wo_ref, g1_ref, b1_ref, w1_ref, w2_ref, g2_ref, b2_ref,
                    y_ref, *, alpha, ff_chunk):
    mix = jnp.dot(m_ref[...], wo_ref[...], preferred_element_type=F32)
    h = _layer_norm(alpha * x_ref[...] + mix, g1_ref[...], b1_ref[...])
    hb = h.astype(BF16)
    d_ff = w1_ref.shape[1]
    ff = jnp.zeros_like(h)
    for c in range(d_ff // ff_chunk):
        cs = slice(c * ff_chunk, (c + 1) * ff_chunk)
        a = jnp.maximum(jnp.dot(hb, w1_ref[:, cs], preferred_element_type=F32), 0.0)
        ff = ff + jnp.dot((a * a).astype(BF16), w2_ref[cs, :], preferred_element_type=F32)
    y_ref[...] = _layer_norm(alpha * h + ff, g2_ref[...], b2_ref[...])


def _mix_ffn(x2d, merged, w_out, ln1_g, ln1_b, w_ff1, w_ff2, ln2_g, ln2_b, alpha, tm):
    n, d = x2d.shape
    d_ff = w_ff1.shape[1]

    def const(shape):
        return pl.BlockSpec(shape, lambda i: (0, 0), pipeline_mode=pl.Buffered(1))

    tile = pl.BlockSpec((tm, d), lambda i: (i, 0))
    kern = functools.partial(_mix_ffn_kernel, alpha=alpha, ff_chunk=min(d_ff, 1024))
    return pl.pallas_call(
        kern,
        grid=(n // tm,),
        in_specs=[tile, tile, const((d, d)), const((1, d)), const((1, d)),
                  const((d, d_ff)), const((d_ff, d)), const((1, d)), const((1, d))],
        out_specs=tile,
        out_shape=jax.ShapeDtypeStruct((n, d), F32),
        compiler_params=pltpu.CompilerParams(
            dimension_semantics=("parallel",), vmem_limit_bytes=VMEM_LIMIT),
        name="mix_ffn",
    )(x2d, merged, w_out, ln1_g, ln1_b, w_ff1, w_ff2, ln2_g, ln2_b)


def _pick_tile(n, target):
    t = min(n, target)
    while n % t:
        t //= 2
    return t


def _layer(x, conv_buf, s_gdn, shift_buf, s_wkv, p):
    b, t_len, d = x.shape
    n = b * t_len
    cb = p["cb"]
    x2d = x.reshape(n, d)

    proj = _proj(x2d, p["w_in"], _pick_tile(n, 1024), p["proj_tn"]).reshape(b, t_len, -1)

    chunk = min(CHUNK, t_len)
    tc = _pick_tile(t_len, TIME_TILE)
    merged, s_gdn_new, s_wkv_new = _mixers(proj, conv_buf, s_gdn, shift_buf[:, None, :], s_wkv, p, tc, chunk)

    y = _mix_ffn(x2d, merged.reshape(n, -1), p["w_out"], p["ln1_g"], p["ln1_b"],
                 p["w_ff1"], p["w_ff2"], p["ln2_g"], p["ln2_b"], p["alpha"], _pick_tile(n, 512))

    conv_cols = conv_buf.shape[-1]
    hist = CONV_W - 1
    assert t_len >= hist
    conv_new = proj[:, t_len - hist:, :conv_cols].astype(x.dtype)
    last = proj[:, t_len - 1]
    shift_new = jnp.concatenate([last[:, cb["br"] * LANES:cb["gate_a"] * LANES],
                                 last[:, cb["lora"] * LANES:cb["ba"] * LANES]], -1).astype(x.dtype)
    return y.reshape(b, t_len, d), conv_new, s_gdn_new, shift_new, s_wkv_new


def _prep_layer(l, depth, w_in, conv_a_w, a_log, dt_bias, gdn_norm_g, mu_b, w0, w_up, a0, a_up, g_up,
                k_k, k_a, r_k, lnx_g, lnx_b, w_out, ln1_g, ln1_b, w_ff1, w_ff2, ln2_g, ln2_b):
    n_heads_a = a_log.shape[1]
    conv_ch = conv_a_w.shape[2]
    d_model = w_out.shape[1]
    v_a = n_heads_a * HEAD_A
    cols_b = mu_b.shape[1]
    small0 = conv_ch + v_a
    small1 = small0 + 2 * n_heads_a
    big_b = cols_b - (w_up.shape[1] + a_up.shape[1] + g_up.shape[1])
    w = w_in[l]
    w_p = jnp.concatenate(
        [w[:, :small0], w[:, small1:small1 + big_b], w[:, small1 + cols_b:], w[:, small1 + big_b:small1 + cols_b],
         jnp.pad(w[:, small0:small1], ((0, 0), (0, LANES - 2 * n_heads_a)))], axis=1).astype(BF16)
    blk = lambda c: c // LANES
    cb = {"q": 0, "k": blk(n_heads_a * HEAD_A), "v": blk(2 * n_heads_a * HEAD_A), "z": blk(conv_ch),
          "br": blk(small0)}
    cb["bk"] = cb["br"] + blk(big_b) // 3
    cb["bv"] = cb["bk"] + blk(big_b) // 3
    cb["gate_a"] = cb["br"] + blk(big_b)
    cb["gate_b"] = cb["gate_a"] + blk(d_model)
    cb["lora"] = cb["gate_b"] + blk(d_model)
    cb["gd"] = cb["lora"] + blk(w_up.shape[1] + a_up.shape[1])
    cb["ba"] = cb["gd"] + blk(g_up.shape[1])
    n_blocks = w_p.shape[1] // LANES
    tn_blocks = max(f for f in range(1, 17) if n_blocks % f == 0)

    def lane_row(vec, offset):
        return jnp.zeros((1, LANES), F32).at[0, offset:offset + vec.shape[0]].set(vec)

    row = lambda a: a[l].reshape(1, -1)
    return {
        "cb": cb, "proj_tn": tn_blocks * LANES, "w_in": w_p, "conv_w": conv_a_w[l],
        "alog_row": lane_row(a_log[l], n_heads_a), "dtb_row": lane_row(dt_bias[l], n_heads_a),
        "gdn_norm_g": row(gdn_norm_g), "mu_b": row(mu_b), "w0": row(w0), "a0": row(a0),
        "k_k": row(k_k), "k_a": row(k_a), "r_k": row(r_k), "lnx_g": row(lnx_g), "lnx_b": row(lnx_b),
        "w_up": w_up[l].astype(BF16), "a_up": a_up[l].astype(BF16), "g_up": g_up[l].astype(BF16),
        "w_out": w_out[l].astype(BF16), "ln1_g": row(ln1_g), "ln1_b": row(ln1_b),
        "w_ff1": w_ff1[l].astype(BF16), "w_ff2": w_ff2[l].astype(BF16),
        "ln2_g": row(ln2_g), "ln2_b": row(ln2_b), "alpha": float((2 * depth) ** 0.25),
    }


def kernel(x_prompt, x_sample, state_conv_a, state_gdn, state_shift_b, state_wkv, w_in, conv_a_w, a_log, dt_bias, gdn_norm_g, mu_b, w0, w_up, a0, a_up, g_up, k_k, k_a, r_k, lnx_g, lnx_b, w_out, ln1_g, ln1_b, w_ff1, w_ff2, ln2_g, ln2_b):
    weights = (w_in, conv_a_w, a_log, dt_bias, gdn_norm_g, mu_b, w0, w_up, a0, a_up, g_up,
               k_k, k_a, r_k, lnx_g, lnx_b, w_out, ln1_g, ln1_b, w_ff1, w_ff2, ln2_g, ln2_b)
    depth = w_in.shape[0]
    bp, dtype = x_prompt.shape[0], x_prompt.dtype
    hp, hs = x_prompt, x_sample
    out_p, out_s = [], []
    for l in range(depth):
        p = _prep_layer(l, depth, *weights)
        zeros = lambda ref: jnp.zeros((bp,) + ref.shape[2:], dtype)
        hp, *st_p = _layer(hp, zeros(state_conv_a), zeros(state_gdn), zeros(state_shift_b),
                           zeros(state_wkv), p)
        hs, *st_s = _layer(hs, state_conv_a[l], state_gdn[l], state_shift_b[l], state_wkv[l], p)
        out_p.append(st_p)
        out_s.append(st_s)
    conv_p, gdn_p, shift_p, wkv_p = (jnp.stack([s[i] for s in out_p]) for i in range(4))
    conv_s, gdn_s, shift_s, wkv_s = (jnp.stack([s[i] for s in out_s]) for i in range(4))
    return (hp, hs, conv_p, gdn_p, shift_p, wkv_p, conv_s, gdn_s, shift_s, wkv_s)
```

```python
import functools
import math

import jax
import jax.numpy as jnp
from jax import lax
from jax.experimental import pallas as pl
from jax.experimental.pallas import tpu as pltpu

F32 = jnp.float32
BF16 = jnp.bfloat16

LANES = 128
MXU_COLS = 256
SUB = 8
BF16_ROWS = 16
HEAD_A = 128
HEAD_B = 64
CONV_W = 4
LN_EPS = 1e-5
NORM_EPS = 1e-6
GN_EPS = 64e-5
DECAY_SCALE = math.exp(-0.5)
CHUNK = 64
TIME_TILE = 256
PROJ_TILE_BLOCKS = 5
VMEM_LIMIT = 56 * 1024 * 1024


def _dot(a, b, dims):
    return lax.dot_general(a.astype(BF16), b.astype(BF16), (dims, ((), ())), preferred_element_type=F32)


def _nn(a, b):
    return _dot(a, b, ((1,), (0,)))


def _nt(a, b):
    return _dot(a, b, ((1,), (1,)))


def _tn(a, b):
    return _dot(a, b, ((0,), (0,)))


def _sigmoid(x):
    return 0.5 * jnp.tanh(0.5 * x) + 0.5


def _silu(x):
    h = 0.5 * x
    return h + h * jnp.tanh(h)


def _softplus(x):
    return jnp.maximum(x, 0.0) + jnp.log1p(jnp.exp(-jnp.abs(x)))


def _layer_norm(x, g, b):
    mu = jnp.mean(x, -1, keepdims=True)
    xc = x - mu
    var = jnp.mean(xc * xc, -1, keepdims=True)
    return xc * lax.rsqrt(var + LN_EPS) * g + b


def _tri_masks(c):
    ii = lax.broadcasted_iota(jnp.int32, (c, c), 0)
    jj = lax.broadcasted_iota(jnp.int32, (c, c), 1)
    return ii >= jj, ii > jj, ii == jj, ii <= jj


def _cumsum_rows(ltri, x):
    n = x.shape[1]
    hi = x.astype(BF16)
    r1 = x - hi.astype(F32)
    mid = r1.astype(BF16)
    lo = (r1 - mid.astype(F32)).astype(BF16)
    res = jnp.dot(ltri, jnp.concatenate([hi, mid, lo], -1), preferred_element_type=F32)
    return res[:, :n] + res[:, n:2 * n] + res[:, 2 * n:]


def _interleave(a, b):
    return [x for pair in zip(a, b) for x in pair]


class _ColumnGroup:
    def __init__(self, buf, slot, first_col, width):
        self.buf, self.slot, self.first_col, self.shape = buf, slot, first_col, (buf.shape[1], width)

    def __getitem__(self, idx):
        rows, cols = idx
        lo = self.first_col + (cols.start or 0)
        hi = self.first_col + (self.shape[1] if cols.stop is None else cols.stop)
        return self.buf[self.slot, rows, lo:hi]


def _mixer_kernel(
        x_cur_ref, x_next_ref, win_ref,
        cq_ref, ck_ref, cv_ref, sg0_ref,
        wq_ref, wk_ref, wv_ref, alog_ref, dtb_ref, ng_ref,
        shr_ref, shk_ref, shv_ref, shl_ref, shg_ref,
        mur_ref, muk_ref, muv_ref, mul_ref, mug_ref,
        w0_ref, a0_ref, kk_ref, ka_ref, rk_ref, lng_ref, lnb_ref,
        wup_ref, aup_ref, gup_ref, sw0_ref,
        o_ref, sg_out_ref, sw_out_ref, tail_ref,
        proj_scr, sg_scr, sw_scr, xq_scr, xk_scr, xv_scr, xr_scr, xbk_scr, xbv_scr, xl_scr, xg_scr,
        *, chunk, n_chunks, cb, proj_tn):
    t = pl.program_id(1)
    step = pl.program_id(0) * pl.num_programs(1) + t
    slot = step % 2
    c_len = chunk
    c2 = 2 * chunk
    tc = chunk * n_chunks
    hist = CONV_W - 1
    nb = HEAD_B
    n_heads = sg_scr.shape[0]
    n_pairs = sw_scr.shape[0]
    per_group = MXU_COLS // LANES
    group_w = per_group * LANES
    d_wide = n_heads * HEAD_A
    n_cols = proj_scr.shape[2]

    def project(x_ref, dst_slot, col_tile):
        cols = slice(col_tile * proj_tn, (col_tile + 1) * proj_tn)
        proj_scr[dst_slot, :, cols] = jnp.dot(x_ref[...].astype(BF16), win_ref[:, cols],
                                              preferred_element_type=F32).astype(BF16)

    @pl.when(step == 0)
    def _():
        for j in range(n_cols // proj_tn):
            project(x_cur_ref, slot, j)

    def group(name, width):
        return _ColumnGroup(proj_scr, slot, cb[name] * LANES, width)

    q_ref, k_ref, v_ref, z_ref, ga_ref = (group(n, d_wide) for n in ("q", "k", "v", "z", "gate_a"))
    r_ref, bk_ref, bv_ref, gb_ref = (group(n, d_wide) for n in ("br", "bk", "bv", "gate_b"))
    l_ref, g_ref, ba_ref = (group(n, LANES) for n in ("lora", "gd", "ba"))

    @pl.when(t == 0)
    def _():
        sg_scr[...] = sg0_ref[...]
        xq_scr[SUB - hist:SUB, :] = cq_ref[...]
        xk_scr[SUB - hist:SUB, :] = ck_ref[...]
        xv_scr[SUB - hist:SUB, :] = cv_ref[...]
        zero = jnp.zeros((nb, nb), F32)
        for p in range(n_pairs):
            sw_scr[p] = jnp.concatenate([jnp.concatenate([sw0_ref[2 * p], zero], 1),
                                         jnp.concatenate([zero, sw0_ref[2 * p + 1]], 1)], 0)
        xr_scr[SUB - 1:SUB, :] = shr_ref[...]
        xbk_scr[SUB - 1:SUB, :] = shk_ref[...]
        xbv_scr[SUB - 1:SUB, :] = shv_ref[...]
        xl_scr[SUB - 1:SUB, :] = shl_ref[...]
        xg_scr[SUB - 1:SUB, :] = shg_ref[...]

    @functools.cache
    def shift_matrix(n_shifts, k_rows, offset):
        rr = lax.broadcasted_iota(jnp.int32, (n_shifts * c_len, k_rows), 0)
        cc = lax.broadcasted_iota(jnp.int32, (n_shifts * c_len, k_rows), 1)
        return (cc == rr % c_len - rr // c_len - 1 + offset).astype(BF16)

    def shifted(x_ref, hs, c, n_shifts, cols):
        if c == 0:
            sh = jnp.dot(shift_matrix(n_shifts, c_len, 0), x_ref[0:c_len, cols], preferred_element_type=F32)
            row = lax.broadcasted_iota(jnp.int32, (SUB, 1), 0)
            h_rows = hs[:, cols]
            out = []
            for j in range(n_shifts):
                blk = sh[j * c_len:(j + 1) * c_len]
                fix = jnp.where(row < j + 1, pltpu.roll(h_rows, j + 1, 0), 0.0)
                out.append(jnp.concatenate([blk[:SUB] + fix, blk[SUB:]], 0) if c_len > SUB else blk + fix)
            return out
        lo = c * c_len - BF16_ROWS
        sh = jnp.dot(shift_matrix(n_shifts, c_len + BF16_ROWS, BF16_ROWS), x_ref[lo:lo + c_len + BF16_ROWS, cols],
                     preferred_element_type=F32)
        return [sh[j * c_len:(j + 1) * c_len] for j in range(n_shifts)]

    lane = lax.broadcasted_iota(jnp.int32, (1, LANES), 1)
    m0 = lane < nb
    lora_w = wup_ref.shape[0]

    def per_head_sum(x):
        s_lo = jnp.sum(jnp.where(m0, x, 0.0), -1, keepdims=True)
        s_hi = jnp.sum(jnp.where(m0, 0.0, x), -1, keepdims=True)
        return jnp.where(m0, s_lo, s_hi)

    incl, strict, eye_m, upper = _tri_masks(c_len)
    eye = eye_m.astype(F32)
    ltri = incl.astype(BF16)
    n_sq = max(c_len.bit_length() - 2, 0)
    ra = lax.broadcasted_iota(jnp.int32, (3 * c_len, c2), 0)
    ca = lax.broadcasted_iota(jnp.int32, (3 * c_len, c2), 1)
    ca_t = jnp.where(ca >= c_len, ca - c_len, ca)
    ra_t = jnp.where(ra < c_len, ra - 1, (ra - c_len) % c_len)
    mask_a = ra_t >= ca_t
    rb_i = lax.broadcasted_iota(jnp.int32, (c_len, c2), 0)
    cb_i = lax.broadcasted_iota(jnp.int32, (c_len, c2), 1)
    first_cols = cb_i < c_len
    mask_b = rb_i > jnp.where(first_cols, cb_i, cb_i - c_len)
    ii2 = lax.broadcasted_iota(jnp.int32, (c2, c2), 0)
    jj2 = lax.broadcasted_iota(jnp.int32, (c2, c2), 1)
    eye2 = (ii2 == jj2).astype(F32)
    si = lax.broadcasted_iota(jnp.int32, (LANES, LANES), 0)
    sj = lax.broadcasted_iota(jnp.int32, (LANES, LANES), 1)
    same_head = (si // nb) == (sj // nb)

    st = [dict() for _ in range(n_chunks)]
    sg = [sg_scr[h] for h in range(n_heads)]
    sw = [sw_scr[p] for p in range(n_pairs)]

    def front(c):
        d = st[c]
        rows = slice(c * c_len, (c + 1) * c_len)

        def conv_step(name, x_ref, hs, w_ref, grp):
            def run():
                cols = slice(grp * group_w, (grp + 1) * group_w)
                w = w_ref[:, cols]
                acc = x_ref[rows, cols].astype(F32) * w[hist:CONV_W, :]
                for j, prev in enumerate(shifted(x_ref, hs, c, hist, cols)):
                    acc = acc + prev * w[hist - 1 - j:hist - j, :]
                d[name, grp] = _silu(acc)
            return run

        def mix_step(name, x_ref, hs, mu_ref, grp):
            def run():
                cols = slice(grp * group_w, (grp + 1) * group_w) if x_ref.shape[1] > group_w else slice(None)
                x = x_ref[rows, cols].astype(F32)
                d[name, grp] = x + mu_ref[:, cols] * (shifted(x_ref, hs, c, 1, cols)[0] - x)
            return run

        def wide(name, h):
            off = h % per_group * LANES
            return d[name, h // per_group][:, off:off + LANES]

        def gates():
            ba = ba_ref[rows, :].astype(F32)
            d["beta_all"] = _sigmoid(ba)
            d["g_all"] = -jnp.exp(alog_ref[...]) * _softplus(ba + dtb_ref[...])
            xl = d["xl", 0]
            u = w0_ref[...] + _nn(jnp.tanh(xl[:, :lora_w]), wup_ref[...])
            d["logw"] = -DECAY_SCALE * _sigmoid(u)
            d["a_all"] = _sigmoid(a0_ref[...] + _nn(xl[:, lora_w:], aup_ref[...]))
            d["gate"] = _nn(_sigmoid(d["xg", 0]), gup_ref[...])

        def gdn_prep(h):
            def run():
                ls = slice(h * HEAD_A, (h + 1) * HEAD_A)
                qh, kh, v = wide("qc", h), wide("kc", h), wide("vc", h)
                q = qh * lax.rsqrt(jnp.sum(qh * qh, -1, keepdims=True) + NORM_EPS) * (HEAD_A ** -0.5)
                k = kh * lax.rsqrt(jnp.sum(kh * kh, -1, keepdims=True) + NORM_EPS)
                beta = jnp.sum(jnp.where(lane == h, d["beta_all"], 0.0), -1, keepdims=True)
                g = jnp.sum(jnp.where(lane == h + n_heads, d["g_all"], 0.0), -1, keepdims=True)
                g_row = jnp.sum(jnp.where(eye_m, g, 0.0), 0, keepdims=True)
                gc = jnp.sum(jnp.where(incl, g_row, 0.0), 1, keepdims=True)
                gc_row = jnp.sum(jnp.where(upper, g, 0.0), 0, keepdims=True)
                eg = jnp.exp(gc)
                g_last = gc[c_len - 1:c_len, :]
                kb = k * beta
                d["g", h] = {
                    "decay": jnp.where(incl, jnp.exp(jnp.where(incl, gc - gc_row, 0.0)), 0.0),
                    "k": k, "kbq": jnp.concatenate([kb, q], 0),
                    "rhs": jnp.concatenate([v * beta, kb * eg], -1), "qe": q * eg,
                    "kd": k * jnp.exp(g_last - gc), "e_last": jnp.exp(g_last)}
            return run

        def rwkv_prep(p):
            def run():
                ls = slice(p * LANES, (p + 1) * LANES)
                r_p, k_p, v_p, a_p = wide("xr", p), wide("xk", p), wide("xv", p), d["a_all"][:, ls]
                kkv = k_p * kk_ref[:, ls]
                kk = kkv * lax.rsqrt(per_head_sum(kkv * kkv) + NORM_EPS)
                k2 = k_p * (1.0 + (a_p - 1.0) * ka_ref[:, ls])
                bvec = kk * a_p
                lw = d["logw"][:, ls]
                cl = _cumsum_rows(ltri, lw)
                cl_last = cl[c_len - 1:c_len, :]
                e_neg = jnp.exp(-cl)
                rt = r_p * jnp.exp(cl)
                kt = k2 * e_neg
                bt = bvec * e_neg
                at = -kk * jnp.exp(cl - lw)
                e_tail = jnp.exp(cl_last - cl)
                at0 = jnp.where(m0, at, 0.0)
                at1 = jnp.where(m0, 0.0, at)
                d["r", p] = {
                    "bonus": per_head_sum(r_p * k2 * rk_ref[:, ls]) * v_p, "v": v_p, "rt": rt,
                    "lhs_a": jnp.concatenate([at0, jnp.where(m0, rt, 0.0), jnp.where(m0, 0.0, rt)], 0),
                    "rhs_a": jnp.concatenate([bt, kt], 0), "at1": at1, "rhs_b": jnp.concatenate([kt, bt], 0),
                    "a01": jnp.concatenate([at0, at1], 0),
                    "v10": jnp.concatenate([jnp.where(m0, 0.0, v_p), jnp.where(m0, v_p, 0.0)], 0),
                    "bk_bar": jnp.concatenate([bvec * e_tail, k2 * e_tail], 0), "w_last": jnp.exp(cl_last)}
            return run

        steps = [(1, mix_step("xl", l_ref, xl_scr, mul_ref, 0)), (1, mix_step("xg", g_ref, xg_scr, mug_ref, 0)),
                 (8, gates)]
        for grp in range(n_heads // per_group):
            steps += [(2, conv_step("qc", q_ref, xq_scr, wq_ref, grp)), (2, conv_step("kc", k_ref, xk_scr, wk_ref, grp)),
                      (2, conv_step("vc", v_ref, xv_scr, wv_ref, grp)),
                      (1, mix_step("xr", r_ref, xr_scr, mur_ref, grp)), (1, mix_step("xk", bk_ref, xbk_scr, muk_ref, grp)),
                      (1, mix_step("xv", bv_ref, xbv_scr, muv_ref, grp))]
            for h in range(grp * per_group, (grp + 1) * per_group):
                steps += [(3, gdn_prep(h)), (4, rwkv_prep(h))]
        return steps

    def mxu(c):
        d = st[c]

        def gram(i):
            def run():
                a, b = d["g", i], d["r", i]
                gm = _nt(a["kbq"], a["k"])
                a["qk"] = gm[c_len:] * a["decay"]
                a["p"] = jnp.where(strict, -gm[:c_len] * a["decay"], 0.0)
                a["t"] = eye + a["p"]
                g_a = jnp.where(mask_a, _nt(b["lhs_a"], b["rhs_a"]), 0.0)
                g_b = jnp.where(mask_b, _nt(b["at1"], b["rhs_b"]), 0.0)
                top = g_a[:c_len]
                b["rbk"] = g_a[c_len:]
                b["p"] = jnp.concatenate([jnp.where(first_cols, top, 0.0), jnp.where(first_cols, 0.0, g_b)], 0)
                b["t"] = eye2 + b["p"]
                b["ak"] = jnp.concatenate([jnp.where(first_cols, 0.0, top), jnp.where(first_cols, g_b, 0.0)], 0)
            return run

        def live_rows(m, lo, blocks):
            return m if lo == 0 else jnp.concatenate([m[b * c_len + lo:(b + 1) * c_len] for b in range(blocks)], 0)

        def zero_rows(k):
            return min(2 ** k // BF16_ROWS * BF16_ROWS, c_len)

        def padded(res, lo, blocks):
            if lo == 0:
                return res
            zero = jnp.zeros((lo, res.shape[1]), F32)
            n = c_len - lo
            return jnp.concatenate([x for b in range(blocks) for x in (zero, res[b * n:(b + 1) * n])], 0)

        def added(t_, add, lo, blocks):
            if lo == 0:
                return t_ + add
            n = c_len - lo
            return jnp.concatenate([x for b in range(blocks)
                                    for x in (t_[b * c_len:b * c_len + lo],
                                              t_[b * c_len + lo:(b + 1) * c_len] + add[b * n:(b + 1) * n])], 0)

        def power_step(i, k, with_t, with_p):
            def run():
                for m, blocks in ((d["g", i], 1), (d["r", i], 2)):
                    lo_t, lo_p = zero_rows(k), zero_rows(k + 1)
                    lhs = ([live_rows(m["t"], lo_t, blocks)] if with_t else []) \
                        + ([live_rows(m["p"], lo_p, blocks)] if with_p else [])
                    res = _nn(lhs[0] if len(lhs) == 1 else jnp.concatenate(lhs, 0), m["p"])
                    n_t = blocks * (c_len - lo_t) if with_t else 0
                    if with_t:
                        m["t"] = added(m["t"], res[:n_t], lo_t, blocks)
                    if with_p:
                        m["p"] = padded(res[n_t:], lo_p, blocks)
            return run

        def wy(i):
            def run():
                a, b = d["g", i], d["r", i]
                sol = _nn(a["t"], a["rhs"])
                a["u"] = sol[:, :HEAD_A]
                a["wq"] = jnp.concatenate([sol[:, HEAD_A:], a["qe"]], 0)
                akv = _nn(b["ak"], b["v10"])
                sol_r = _nn(b["t"], jnp.concatenate([b["a01"], akv], 1))
                b["wr"] = jnp.concatenate([sol_r[:c_len, :LANES] + sol_r[c_len:, :LANES], b["rt"]], 0)
                b["u"] = sol_r[:c_len, LANES:] + sol_r[c_len:, LANES:]
            return run

        def read_state(i):
            def run():
                a, b = d["g", i], d["r", i]
                a["ws"] = _nn(a["wq"], sg[i])
                b["ws"] = _nt(b["wr"], sw[i])
            return run

        def update_state(i):
            def run():
                a, b = d["g", i], d["r", i]
                a["v_new"] = a["u"] - a["ws"][:c_len]
                b["pv"] = jnp.concatenate([b["u"] + b["ws"][:c_len], b["v"]], 0)
                sg[i] = sg[i] * a["e_last"] + _tn(a["kd"], a["v_new"])
                sw[i] = sw[i] * b["w_last"] + jnp.where(same_head, _tn(b["pv"], b["bk_bar"]), 0.0)
            return run

        def outputs(i):
            def run():
                a, b = d["g", i], d["r", i]
                a["o"] = a["ws"][c_len:] + _nn(a["qk"], a["v_new"])
                y_ = _nn(b["rbk"], b["pv"])
                b["y"] = b["ws"][c_len:] + jnp.where(m0, y_[:c_len], y_[c_len:])
            return run

        heads = range(n_heads)
        steps = [(6, gram(i)) for i in heads]
        for k in range(n_sq + 1):
            if n_sq > 0:
                steps += [(3 if k in (0, n_sq) else 6, power_step(i, k, k > 0, k < n_sq)) for i in heads]
        steps += [(5, wy(i)) for i in heads] + [(4, read_state(i)) for i in heads]
        return steps + [(4, update_state(i)) for i in heads] + [(3, outputs(i)) for i in heads]

    def back(c):
        d = st[c]
        rows = slice(c * c_len, (c + 1) * c_len)

        def head(h):
            def run():
                ls = slice(h * LANES, (h + 1) * LANES)
                o = d["g", h]["o"]
                o = o * lax.rsqrt(jnp.mean(o * o, -1, keepdims=True) + NORM_EPS) * ng_ref[...]
                z = z_ref[rows, ls].astype(F32)
                o_a = o * _silu(z)
                y = d["r", h]["y"]
                mu = per_head_sum(y) * (1.0 / nb)
                yc = y - mu
                var = per_head_sum(yc * yc) * (1.0 / nb)
                y = yc * lax.rsqrt(var + GN_EPS) * lng_ref[:, ls] + lnb_ref[:, ls] + d["r", h]["bonus"]
                o_b = y * d["gate"][:, ls]
                o_ref[rows, ls] = (_sigmoid(ga_ref[rows, ls].astype(F32)) * o_a
                                   + _sigmoid(gb_ref[rows, ls].astype(F32)) * o_b).astype(o_ref.dtype)
            return run

        return [(1, head(h)) for h in range(n_heads)]

    def run_merged(*lists):
        order = []
        for k, steps in enumerate(lists):
            total, done = sum(w for w, _ in steps), 0.0
            for i, (w, _) in enumerate(steps):
                order.append(((done + 0.5 * w) / total, k, i))
                done += w
        for _, k, i in sorted(order):
            lists[k][i][1]()

    n_col_tiles = n_cols // proj_tn

    def project_next(c):
        def tile(j):
            return lambda: project(x_next_ref, 1 - slot, j)
        mine = range(c * n_col_tiles // n_chunks, (c + 1) * n_col_tiles // n_chunks)
        return [(1, tile(j)) for j in mine]

    run_merged(front(0))
    for c in range(n_chunks):
        stages = [mxu(c), project_next(c)]
        if c + 1 < n_chunks:
            stages.append(front(c + 1))
        if c > 0:
            stages.append(back(c - 1))
        run_merged(*[s_ for s_ in stages if s_])
    run_merged(back(n_chunks - 1))

    for h in range(n_heads):
        sg_scr[h] = sg[h]
        sw_scr[h] = sw[h]
    for x_ref, hs in ((q_ref, xq_scr), (k_ref, xk_scr), (v_ref, xv_scr), (r_ref, xr_scr), (bk_ref, xbk_scr),
                      (bv_ref, xbv_scr), (l_ref, xl_scr), (g_ref, xg_scr)):
        hs[...] = x_ref[tc - SUB:tc, :].astype(F32)

    @pl.when(t == pl.num_programs(1) - 1)
    def _():
        tail_ref[...] = proj_scr[slot, tc - BF16_ROWS:tc, :]
        sg_out_ref[...] = sg_scr[...]
        for p in range(n_pairs):
            s_p = sw_scr[p]
            sw_out_ref[2 * p] = s_p[:nb, :nb]
            sw_out_ref[2 * p + 1] = s_p[nb:, nb:]


def _mixers(x, conv_state, s_gdn, shift_state, s_wkv, p, tc, chunk):
    b, t_len, d_model = x.shape
    cb = p["cb"]
    w_in = p["w_in"]
    n_cols = w_in.shape[1]
    n_heads = s_gdn.shape[1]
    d = n_heads * HEAD_A
    n_pairs = s_wkv.shape[1] * HEAD_B // LANES
    assert n_pairs == n_heads and d == n_pairs * LANES
    n_t = t_len // tc
    per = d // LANES
    small = 3 * per
    tiles = x.reshape(b * n_t, tc, d_model)
    last_tile = b * n_t - 1

    def ccol(g):
        return pl.BlockSpec((None, CONV_W - 1, d), lambda bi, ti: (bi, 0, g))

    def wcol(g):
        return pl.BlockSpec((CONV_W, d), lambda bi, ti: (0, g))

    def scol(g):
        return pl.BlockSpec((None, 1, d), lambda bi, ti: (bi, 0, g))

    def sfix(blk):
        return pl.BlockSpec((None, 1, LANES), lambda bi, ti: (bi, 0, blk))

    def mcol(g):
        return pl.BlockSpec((1, d), lambda bi, ti: (0, g))

    def mfix(blk):
        return pl.BlockSpec((1, LANES), lambda bi, ti: (0, blk))

    def full(a):
        return pl.BlockSpec(a.shape, lambda bi, ti: (0,) * a.ndim)

    row = pl.BlockSpec((1, LANES), lambda bi, ti: (0, 0))
    st_g = pl.BlockSpec((None,) + s_gdn.shape[1:], lambda bi, ti: (bi, 0, 0, 0))
    st_w = pl.BlockSpec((None,) + s_wkv.shape[1:], lambda bi, ti: (bi, 0, 0, 0))
    prows = [p[n] for n in ("w0", "a0", "k_k", "k_a", "r_k", "lnx_g", "lnx_b", "w_up", "a_up", "g_up")]
    kern = functools.partial(_mixer_kernel, chunk=chunk, n_chunks=tc // chunk, cb=cb, proj_tn=p["proj_tn"])
    wide = pltpu.VMEM((SUB, d), F32)
    narrow = pltpu.VMEM((SUB, LANES), F32)
    x_tile = (None, tc, d_model)
    return pl.pallas_call(
        kern,
        grid=(b, n_t),
        in_specs=[pl.BlockSpec(x_tile, lambda bi, ti: (bi * n_t + ti, 0, 0)),
                  pl.BlockSpec(x_tile, lambda bi, ti: (jnp.minimum(bi * n_t + ti + 1, last_tile), 0, 0)),
                  pl.BlockSpec(w_in.shape, lambda bi, ti: (0, 0), pipeline_mode=pl.Buffered(1)),
                  ccol(0), ccol(1), ccol(2), st_g, wcol(0), wcol(1), wcol(2), row, row, row,
                  scol(0), scol(1), scol(2), sfix(small), sfix(small + 1),
                  mcol(0), mcol(1), mcol(2), mfix(small), mfix(small + 1)]
                 + [full(a) for a in prows] + [st_w],
        out_specs=[pl.BlockSpec((None, tc, d), lambda bi, ti: (bi, ti, 0)), st_g, st_w,
                   pl.BlockSpec((None, BF16_ROWS, n_cols), lambda bi, ti: (bi, 0, 0))],
        out_shape=[jax.ShapeDtypeStruct((b, t_len, d), BF16),
                   jax.ShapeDtypeStruct(s_gdn.shape, F32), jax.ShapeDtypeStruct(s_wkv.shape, F32),
                   jax.ShapeDtypeStruct((b, BF16_ROWS, n_cols), BF16)],
        scratch_shapes=[pltpu.VMEM((2, tc, n_cols), BF16),
                        pltpu.VMEM((n_heads, HEAD_A, HEAD_A), F32), pltpu.VMEM((n_pairs, LANES, LANES), F32),
                        wide, wide, wide, wide, wide, wide, narrow, narrow],
        compiler_params=pltpu.CompilerParams(
            dimension_semantics=("arbitrary", "arbitrary"), vmem_limit_bytes=VMEM_LIMIT),
        name="mixers",
    )(tiles, tiles, w_in, conv_state, conv_state, conv_state, s_gdn,
      p["conv_w"], p["conv_w"], p["conv_w"], p["alog_row"], p["dtb_row"], p["gdn_norm_g"],
      shift_state, shift_state, shift_state, shift_state, shift_state,
      p["mu_b"], p["mu_b"], p["mu_b"], p["mu_b"], p["mu_b"], *prows, s_wkv)


def _mix_ffn_kernel(x_ref, m_ref, wo_ref, g1_ref, b1_ref, w1_ref, w2_ref, g2_ref, b2_ref,
                    y_ref, *, alpha, ff_chunk):
    mix = jnp.dot(m_ref[...], wo_ref[...], preferred_element_type=F32)
    h = _layer_norm(alpha * x_ref[...] + mix, g1_ref[...], b1_ref[...])
    hb = h.astype(BF16)
    d_ff = w1_ref.shape[1]
    ff = jnp.zeros_like(h)
    for c in range(d_ff // ff_chunk):
        cs = slice(c * ff_chunk, (c + 1) * ff_chunk)
        a = jnp.maximum(jnp.dot(hb, w1_ref[:, cs], preferred_element_type=F32), 0.0)
        ff = ff + jnp.dot((a * a).astype(BF16), w2_ref[cs, :], preferred_element_type=F32)
    y_ref[...] = _layer_norm(alpha * h + ff, g2_ref[...], b2_ref[...])


def _mix_ffn(x2d, merged, w_out, ln1_g, ln1_b, w_ff1, w_ff2, ln2_g, ln2_b, alpha, tm):
    n, d = x2d.shape
    d_ff = w_ff1.shape[1]

    def const(shape):
        return pl.BlockSpec(shape, lambda i: (0, 0), pipeline_mode=pl.Buffered(1))

    tile = pl.BlockSpec((tm, d), lambda i: (i, 0))
    kern = functools.partial(_mix_ffn_kernel, alpha=alpha, ff_chunk=min(d_ff, 1024))
    return pl.pallas_call(
        kern,
        grid=(n // tm,),
        in_specs=[tile, tile, const((d, d)), const((1, d)), const((1, d)),
                  const((d, d_ff)), const((d_ff, d)), const((1, d)), const((1, d))],
        out_specs=tile,
        out_shape=jax.ShapeDtypeStruct((n, d), F32),
        compiler_params=pltpu.CompilerParams(
            dimension_semantics=("parallel",), vmem_limit_bytes=VMEM_LIMIT),
        name="mix_ffn",
    )(x2d, merged, w_out, ln1_g, ln1_b, w_ff1, w_ff2, ln2_g, ln2_b)


def _pick_tile(n, target):
    t = min(n, target)
    while n % t:
        t //= 2
    return t


def _layer(x, conv_buf, s_gdn, shift_buf, s_wkv, p):
    b, t_len, d = x.shape
    n = b * t_len
    cb = p["cb"]
    x2d = x.reshape(n, d)

    chunk = min(CHUNK, t_len)
    tc = _pick_tile(t_len, TIME_TILE)
    assert tc >= BF16_ROWS
    merged, s_gdn_new, s_wkv_new, tail = _mixers(x, conv_buf, s_gdn, shift_buf[:, None, :], s_wkv, p, tc, chunk)

    y = _mix_ffn(x2d, merged.reshape(n, -1), p["w_out"], p["ln1_g"], p["ln1_b"],
                 p["w_ff1"], p["w_ff2"], p["ln2_g"], p["ln2_b"], p["alpha"], _pick_tile(n, 512))

    conv_cols = conv_buf.shape[-1]
    hist = CONV_W - 1
    assert t_len >= hist
    conv_new = tail[:, BF16_ROWS - hist:, :conv_cols].astype(x.dtype)
    last = tail[:, BF16_ROWS - 1]
    shift_new = jnp.concatenate([last[:, cb["br"] * LANES:cb["gate_a"] * LANES],
                                 last[:, cb["lora"] * LANES:cb["ba"] * LANES]], -1).astype(x.dtype)
    return y.reshape(b, t_len, d), conv_new, s_gdn_new, shift_new, s_wkv_new


def _prep_layer(l, depth, w_in, conv_a_w, a_log, dt_bias, gdn_norm_g, mu_b, w0, w_up, a0, a_up, g_up,
                k_k, k_a, r_k, lnx_g, lnx_b, w_out, ln1_g, ln1_b, w_ff1, w_ff2, ln2_g, ln2_b):
    n_heads_a = a_log.shape[1]
    conv_ch = conv_a_w.shape[2]
    d_model = w_out.shape[1]
    v_a = n_heads_a * HEAD_A
    cols_b = mu_b.shape[1]
    small0 = conv_ch + v_a
    small1 = small0 + 2 * n_heads_a
    big_b = cols_b - (w_up.shape[1] + a_up.shape[1] + g_up.shape[1])
    w = w_in[l]
    w_p = jnp.concatenate(
        [w[:, :small0], w[:, small1:small1 + big_b], w[:, small1 + cols_b:], w[:, small1 + big_b:small1 + cols_b],
         jnp.pad(w[:, small0:small1], ((0, 0), (0, LANES - 2 * n_heads_a)))], axis=1).astype(BF16)
    blk = lambda c: c // LANES
    cb = {"q": 0, "k": blk(n_heads_a * HEAD_A), "v": blk(2 * n_heads_a * HEAD_A), "z": blk(conv_ch),
          "br": blk(small0)}
    cb["bk"] = cb["br"] + blk(big_b) // 3
    cb["bv"] = cb["bk"] + blk(big_b) // 3
    cb["gate_a"] = cb["br"] + blk(big_b)
    cb["gate_b"] = cb["gate_a"] + blk(d_model)
    cb["lora"] = cb["gate_b"] + blk(d_model)
    cb["gd"] = cb["lora"] + blk(w_up.shape[1] + a_up.shape[1])
    cb["ba"] = cb["gd"] + blk(g_up.shape[1])
    n_blocks = w_p.shape[1] // LANES
    tn_blocks = max(f for f in range(1, PROJ_TILE_BLOCKS + 1) if n_blocks % f == 0)

    def lane_row(vec, offset):
        return jnp.zeros((1, LANES), F32).at[0, offset:offset + vec.shape[0]].set(vec)

    row = lambda a: a[l].reshape(1, -1)
    return {
        "cb": cb, "proj_tn": tn_blocks * LANES, "w_in": w_p, "conv_w": conv_a_w[l],
        "alog_row": lane_row(a_log[l], n_heads_a), "dtb_row": lane_row(dt_bias[l], n_heads_a),
        "gdn_norm_g": row(gdn_norm_g), "mu_b": row(mu_b), "w0": row(w0), "a0": row(a0),
        "k_k": row(k_k), "k_a": row(k_a), "r_k": row(r_k), "lnx_g": row(lnx_g), "lnx_b": row(lnx_b),
        "w_up": w_up[l].astype(BF16), "a_up": a_up[l].astype(BF16), "g_up": g_up[l].astype(BF16),
        "w_out": w_out[l].astype(BF16), "ln1_g": row(ln1_g), "ln1_b": row(ln1_b),
        "w_ff1": w_ff1[l].astype(BF16), "w_ff2": w_ff2[l].astype(BF16),
        "ln2_g": row(ln2_g), "ln2_b": row(ln2_b), "alpha": float((2 * depth) ** 0.25),
    }


def kernel(x_prompt, x_sample, state_conv_a, state_gdn, state_shift_b, state_wkv, w_in, conv_a_w, a_log, dt_bias, gdn_norm_g, mu_b, w0, w_up, a0, a_up, g_up, k_k, k_a, r_k, lnx_g, lnx_b, w_out, ln1_g, ln1_b, w_ff1, w_ff2, ln2_g, ln2_b):
    weights = (w_in, conv_a_w, a_log, dt_bias, gdn_norm_g, mu_b, w0, w_up, a0, a_up, g_up,
               k_k, k_a, r_k, lnx_g, lnx_b, w_out, ln1_g, ln1_b, w_ff1, w_ff2, ln2_g, ln2_b)
    depth = w_in.shape[0]
    bp, dtype = x_prompt.shape[0], x_prompt.dtype
    hp, hs = x_prompt, x_sample
    out_p, out_s = [], []
    for l in range(depth):
        p = _prep_layer(l, depth, *weights)
        zeros = lambda ref: jnp.zeros((bp,) + ref.shape[2:], dtype)
        hp, *st_p = _layer(hp, zeros(state_conv_a), zeros(state_gdn), zeros(state_shift_b),
                           zeros(state_wkv), p)
        hs, *st_s = _layer(hs, state_conv_a[l], state_gdn[l], state_shift_b[l], state_wkv[l], p)
        out_p.append(st_p)
        out_s.append(st_s)
    conv_p, gdn_p, shift_p, wkv_p = (jnp.stack([s[i] for s in out_p]) for i in range(4))
    conv_s, gdn_s, shift_s, wkv_s = (jnp.stack([s[i] for s in out_s]) for i in range(4))
    return (hp, hs, conv_p, gdn_p, shift_p, wkv_p, conv_s, gdn_s, shift_s, wkv_s)
```

```python
import functools
import math

import jax
import jax.numpy as jnp
from jax import lax
from jax.experimental import pallas as pl
from jax.experimental.pallas import tpu as pltpu

F32 = jnp.float32
BF16 = jnp.bfloat16

LANES = 128
MXU_COLS = 256
SUB = 8
BF16_ROWS = 16
HEAD_A = 128
HEAD_B = 64
CONV_W = 4
LN_EPS = 1e-5
NORM_EPS = 1e-6
GN_EPS = 64e-5
DECAY_SCALE = math.exp(-0.5)
CHUNK = 64
TIME_TILE = 256
PROJ_TILE_BLOCKS = 5
VMEM_LIMIT = 56 * 1024 * 1024


def _dot(a, b, dims):
    return lax.dot_general(a.astype(BF16), b.astype(BF16), (dims, ((), ())), preferred_element_type=F32)


def _nn(a, b):
    return _dot(a, b, ((1,), (0,)))


def _nt(a, b):
    return _dot(a, b, ((1,), (1,)))


def _tn(a, b):
    return _dot(a, b, ((0,), (0,)))


def _sigmoid(x):
    return 0.5 * jnp.tanh(0.5 * x) + 0.5


def _silu(x):
    h = 0.5 * x
    return h + h * jnp.tanh(h)


def _softplus(x):
    return jnp.maximum(x, 0.0) + jnp.log1p(jnp.exp(-jnp.abs(x)))


def _layer_norm(x, g, b):
    mu = jnp.mean(x, -1, keepdims=True)
    xc = x - mu
    var = jnp.mean(xc * xc, -1, keepdims=True)
    return xc * lax.rsqrt(var + LN_EPS) * g + b


def _tri_masks(c):
    ii = lax.broadcasted_iota(jnp.int32, (c, c), 0)
    jj = lax.broadcasted_iota(jnp.int32, (c, c), 1)
    return ii >= jj, ii > jj, ii == jj, ii <= jj


def _cumsum_rows(ltri, x):
    n = x.shape[1]
    hi = x.astype(BF16)
    r1 = x - hi.astype(F32)
    mid = r1.astype(BF16)
    lo = (r1 - mid.astype(F32)).astype(BF16)
    res = jnp.dot(ltri, jnp.concatenate([hi, mid, lo], -1), preferred_element_type=F32)
    return res[:, :n] + res[:, n:2 * n] + res[:, 2 * n:]


def _interleave(a, b):
    return [x for pair in zip(a, b) for x in pair]


class _ColumnGroup:
    def __init__(self, buf, slot, first_col, width):
        self.buf, self.slot, self.first_col, self.shape = buf, slot, first_col, (buf.shape[1], width)

    def __getitem__(self, idx):
        rows, cols = idx
        lo = self.first_col + (cols.start or 0)
        hi = self.first_col + (self.shape[1] if cols.stop is None else cols.stop)
        return self.buf[self.slot, rows, lo:hi]


def _mixer_kernel(
        x_cur_ref, x_next_ref, win_ref,
        cq_ref, ck_ref, cv_ref, sg0_ref,
        wq_ref, wk_ref, wv_ref, alog_ref, dtb_ref, ng_ref,
        shr_ref, shk_ref, shv_ref, shl_ref, shg_ref,
        mur_ref, muk_ref, muv_ref, mul_ref, mug_ref,
        w0_ref, a0_ref, kk_ref, ka_ref, rk_ref, lng_ref, lnb_ref,
        wup_ref, aup_ref, gup_ref, sw0_ref,
        o_ref, sg_out_ref, sw_out_ref, tail_ref,
        proj_scr, sg_scr, sw_scr, xq_scr, xk_scr, xv_scr, xr_scr, xbk_scr, xbv_scr, xl_scr, xg_scr,
        *, chunk, n_chunks, cb, proj_tn):
    t = pl.program_id(1)
    step = pl.program_id(0) * pl.num_programs(1) + t
    slot = step % 2
    c_len = chunk
    c2 = 2 * chunk
    tc = chunk * n_chunks
    hist = CONV_W - 1
    nb = HEAD_B
    n_heads = sg_scr.shape[0]
    n_pairs = sw_scr.shape[0]
    per_group = MXU_COLS // LANES
    group_w = per_group * LANES
    d_wide = n_heads * HEAD_A
    n_cols = proj_scr.shape[2]

    def project(x_ref, dst_slot, col_tile):
        cols = slice(col_tile * proj_tn, (col_tile + 1) * proj_tn)
        proj_scr[dst_slot, :, cols] = jnp.dot(x_ref[...].astype(BF16), win_ref[:, cols],
                                              preferred_element_type=F32).astype(BF16)

    @pl.when(step == 0)
    def _():
        for j in range(n_cols // proj_tn):
            project(x_cur_ref, slot, j)

    def group(name, width):
        return _ColumnGroup(proj_scr, slot, cb[name] * LANES, width)

    q_ref, k_ref, v_ref, z_ref, ga_ref = (group(n, d_wide) for n in ("q", "k", "v", "z", "gate_a"))
    r_ref, bk_ref, bv_ref, gb_ref = (group(n, d_wide) for n in ("br", "bk", "bv", "gate_b"))
    l_ref, g_ref, ba_ref = (group(n, LANES) for n in ("lora", "gd", "ba"))

    @pl.when(t == 0)
    def _():
        sg_scr[...] = sg0_ref[...]
        xq_scr[SUB - hist:SUB, :] = cq_ref[...]
        xk_scr[SUB - hist:SUB, :] = ck_ref[...]
        xv_scr[SUB - hist:SUB, :] = cv_ref[...]
        zero = jnp.zeros((nb, nb), F32)
        for p in range(n_pairs):
            sw_scr[p] = jnp.concatenate([jnp.concatenate([sw0_ref[2 * p], zero], 1),
                                         jnp.concatenate([zero, sw0_ref[2 * p + 1]], 1)], 0)
        xr_scr[SUB - 1:SUB, :] = shr_ref[...]
        xbk_scr[SUB - 1:SUB, :] = shk_ref[...]
        xbv_scr[SUB - 1:SUB, :] = shv_ref[...]
        xl_scr[SUB - 1:SUB, :] = shl_ref[...]
        xg_scr[SUB - 1:SUB, :] = shg_ref[...]

    def shifted(x_ref, hs, c, n_shifts, cols, x):
        if c == 0:
            prev = hs[:, cols]
        else:
            lo = c * c_len - BF16_ROWS
            prev = x_ref[lo:lo + BF16_ROWS, cols].astype(F32)[BF16_ROWS - SUB:]
        xw = jnp.concatenate([prev, x], 0)
        return [xw[SUB - 1 - j:SUB - 1 - j + c_len] for j in range(n_shifts)]

    lane = lax.broadcasted_iota(jnp.int32, (1, LANES), 1)
    m0 = lane < nb
    lora_w = wup_ref.shape[0]

    def per_head_sum(x):
        s_lo = jnp.sum(jnp.where(m0, x, 0.0), -1, keepdims=True)
        s_hi = jnp.sum(jnp.where(m0, 0.0, x), -1, keepdims=True)
        return jnp.where(m0, s_lo, s_hi)

    incl, strict, eye_m, upper = _tri_masks(c_len)
    eye = eye_m.astype(F32)
    ltri = incl.astype(BF16)
    n_sq = max(c_len.bit_length() - 2, 0)
    ra = lax.broadcasted_iota(jnp.int32, (3 * c_len, c2), 0)
    ca = lax.broadcasted_iota(jnp.int32, (3 * c_len, c2), 1)
    ca_t = jnp.where(ca >= c_len, ca - c_len, ca)
    ra_t = jnp.where(ra < c_len, ra - 1, (ra - c_len) % c_len)
    mask_a = ra_t >= ca_t
    rb_i = lax.broadcasted_iota(jnp.int32, (c_len, c2), 0)
    cb_i = lax.broadcasted_iota(jnp.int32, (c_len, c2), 1)
    first_cols = cb_i < c_len
    mask_b = rb_i > jnp.where(first_cols, cb_i, cb_i - c_len)
    ii2 = lax.broadcasted_iota(jnp.int32, (c2, c2), 0)
    jj2 = lax.broadcasted_iota(jnp.int32, (c2, c2), 1)
    eye2 = (ii2 == jj2).astype(F32)
    si = lax.broadcasted_iota(jnp.int32, (LANES, LANES), 0)
    sj = lax.broadcasted_iota(jnp.int32, (LANES, LANES), 1)
    same_head = (si // nb) == (sj // nb)

    st = [dict() for _ in range(n_chunks)]
    sg = [sg_scr[h] for h in range(n_heads)]
    sw = [sw_scr[p] for p in range(n_pairs)]

    def front(c, part):
        d = st[c]
        rows = slice(c * c_len, (c + 1) * c_len)

        def conv_step(name, x_ref, hs, w_ref, grp):
            def run():
                cols = slice(grp * group_w, (grp + 1) * group_w)
                w = w_ref[:, cols]
                x = x_ref[rows, cols].astype(F32)
                acc = x * w[hist:CONV_W, :]
                for j, prev in enumerate(shifted(x_ref, hs, c, hist, cols, x)):
                    acc = acc + prev * w[hist - 1 - j:hist - j, :]
                d[name, grp] = _silu(acc)
            return run

        def mix_step(name, x_ref, hs, mu_ref, grp):
            def run():
                cols = slice(grp * group_w, (grp + 1) * group_w) if x_ref.shape[1] > group_w else slice(None)
                x = x_ref[rows, cols].astype(F32)
                d[name, grp] = x + mu_ref[:, cols] * (shifted(x_ref, hs, c, 1, cols, x)[0] - x)
            return run

        def wide(name, h):
            off = h % per_group * LANES
            return d[name, h // per_group][:, off:off + LANES]

        def gates_g():
            ba = ba_ref[rows, :].astype(F32)
            d["beta_all"] = _sigmoid(ba)
            d["g_all"] = -jnp.exp(alog_ref[...]) * _softplus(ba + dtb_ref[...])

        def gates_r():
            xl = d["xl", 0]
            u = w0_ref[...] + _nn(jnp.tanh(xl[:, :lora_w]), wup_ref[...])
            d["logw"] = -DECAY_SCALE * _sigmoid(u)
            d["a_all"] = _sigmoid(a0_ref[...] + _nn(xl[:, lora_w:], aup_ref[...]))
            d["gate"] = _nn(_sigmoid(d["xg", 0]), gup_ref[...])

        def gdn_prep(h):
            def run():
                ls = slice(h * HEAD_A, (h + 1) * HEAD_A)
                qh, kh, v = wide("qc", h), wide("kc", h), wide("vc", h)
                q = qh * lax.rsqrt(jnp.sum(qh * qh, -1, keepdims=True) + NORM_EPS) * (HEAD_A ** -0.5)
                k = kh * lax.rsqrt(jnp.sum(kh * kh, -1, keepdims=True) + NORM_EPS)
                beta = jnp.sum(jnp.where(lane == h, d["beta_all"], 0.0), -1, keepdims=True)
                g = jnp.sum(jnp.where(lane == h + n_heads, d["g_all"], 0.0), -1, keepdims=True)
                g_row = jnp.sum(jnp.where(eye_m, g, 0.0), 0, keepdims=True)
                gc = jnp.sum(jnp.where(incl, g_row, 0.0), 1, keepdims=True)
                gc_row = jnp.sum(jnp.where(upper, g, 0.0), 0, keepdims=True)
                eg = jnp.exp(gc)
                g_last = gc[c_len - 1:c_len, :]
                kb = k * beta
                d["g", h] = {
                    "decay": jnp.where(incl, jnp.exp(jnp.where(incl, gc - gc_row, 0.0)), 0.0),
                    "k": k, "kbq": jnp.concatenate([kb, q], 0),
                    "rhs": jnp.concatenate([v * beta, kb * eg], -1), "qe": q * eg,
                    "kd": k * jnp.exp(g_last - gc), "e_last": jnp.exp(g_last)}
            return run

        def rwkv_prep(p):
            def run():
                ls = slice(p * LANES, (p + 1) * LANES)
                r_p, k_p, v_p, a_p = wide("xr", p), wide("xk", p), wide("xv", p), d["a_all"][:, ls]
                kkv = k_p * kk_ref[:, ls]
                kk = kkv * lax.rsqrt(per_head_sum(kkv * kkv) + NORM_EPS)
                k2 = k_p * (1.0 + (a_p - 1.0) * ka_ref[:, ls])
                bvec = kk * a_p
                lw = d["logw"][:, ls]
                cl = _cumsum_rows(ltri, lw)
                cl_last = cl[c_len - 1:c_len, :]
                e_neg = jnp.exp(-cl)
                rt = r_p * jnp.exp(cl)
                kt = k2 * e_neg
                bt = bvec * e_neg
                at = -kk * jnp.exp(cl - lw)
                e_tail = jnp.exp(cl_last - cl)
                at0 = jnp.where(m0, at, 0.0)
                at1 = jnp.where(m0, 0.0, at)
                d["r", p] = {
                    "bonus": per_head_sum(r_p * k2 * rk_ref[:, ls]) * v_p, "v": v_p, "rt": rt,
                    "lhs_a": jnp.concatenate([at0, jnp.where(m0, rt, 0.0), jnp.where(m0, 0.0, rt)], 0),
                    "rhs_a": jnp.concatenate([bt, kt], 0), "at1": at1, "rhs_b": jnp.concatenate([kt, bt], 0),
                    "a01": jnp.concatenate([at0, at1], 0),
                    "v10": jnp.concatenate([jnp.where(m0, 0.0, v_p), jnp.where(m0, v_p, 0.0)], 0),
                    "bk_bar": jnp.concatenate([bvec * e_tail, k2 * e_tail], 0), "w_last": jnp.exp(cl_last)}
            return run

        if part == "g":
            steps = [(1, gates_g)]
            for grp in range(n_heads // per_group):
                steps += [(2, conv_step("qc", q_ref, xq_scr, wq_ref, grp)),
                          (2, conv_step("kc", k_ref, xk_scr, wk_ref, grp)),
                          (2, conv_step("vc", v_ref, xv_scr, wv_ref, grp))]
                steps += [(3, gdn_prep(h)) for h in range(grp * per_group, (grp + 1) * per_group)]
            return steps
        steps = [(1, mix_step("xl", l_ref, xl_scr, mul_ref, 0)), (1, mix_step("xg", g_ref, xg_scr, mug_ref, 0)),
                 (7, gates_r)]
        for grp in range(n_heads // per_group):
            steps += [(1, mix_step("xr", r_ref, xr_scr, mur_ref, grp)), (1, mix_step("xk", bk_ref, xbk_scr, muk_ref, grp)),
                      (1, mix_step("xv", bv_ref, xbv_scr, muv_ref, grp))]
            steps += [(4, rwkv_prep(h)) for h in range(grp * per_group, (grp + 1) * per_group)]
        return steps

    def mxu(c, part):
        d = st[c]
        g_side = part == "g"

        def gram(i):
            def run():
                a, b = d.get(("g", i)), d.get(("r", i))
                if g_side:
                    gm = _nt(a["kbq"], a["k"])
                    a["qk"] = gm[c_len:] * a["decay"]
                    a["p"] = jnp.where(strict, -gm[:c_len] * a["decay"], 0.0)
                    a["t"] = eye + a["p"]
                    return
                g_a = jnp.where(mask_a, _nt(b["lhs_a"], b["rhs_a"]), 0.0)
                g_b = jnp.where(mask_b, _nt(b["at1"], b["rhs_b"]), 0.0)
                top = g_a[:c_len]
                b["rbk"] = g_a[c_len:]
                b["p"] = jnp.concatenate([jnp.where(first_cols, top, 0.0), jnp.where(first_cols, 0.0, g_b)], 0)
                b["t"] = eye2 + b["p"]
                b["ak"] = jnp.concatenate([jnp.where(first_cols, 0.0, top), jnp.where(first_cols, g_b, 0.0)], 0)
            return run

        def live_rows(m, lo, blocks):
            return m if lo == 0 else jnp.concatenate([m[b * c_len + lo:(b + 1) * c_len] for b in range(blocks)], 0)

        def zero_rows(k):
            return min(2 ** k // BF16_ROWS * BF16_ROWS, c_len)

        def padded(res, lo, blocks):
            if lo == 0:
                return res
            zero = jnp.zeros((lo, res.shape[1]), F32)
            n = c_len - lo
            return jnp.concatenate([x for b in range(blocks) for x in (zero, res[b * n:(b + 1) * n])], 0)

        def added(t_, add, lo, blocks):
            if lo == 0:
                return t_ + add
            n = c_len - lo
            return jnp.concatenate([x for b in range(blocks)
                                    for x in (t_[b * c_len:b * c_len + lo],
                                              t_[b * c_len + lo:(b + 1) * c_len] + add[b * n:(b + 1) * n])], 0)

        def power_step(i, k, with_t, with_p):
            def run():
                for m, blocks in (((d["g", i], 1),) if g_side else ((d["r", i], 2),)):
                    lo_t, lo_p = zero_rows(k), zero_rows(k + 1)
                    lhs = ([live_rows(m["t"], lo_t, blocks)] if with_t else []) \
                        + ([live_rows(m["p"], lo_p, blocks)] if with_p else [])
                    res = _nn(lhs[0] if len(lhs) == 1 else jnp.concatenate(lhs, 0), m["p"])
                    n_t = blocks * (c_len - lo_t) if with_t else 0
                    if with_t:
                        m["t"] = added(m["t"], res[:n_t], lo_t, blocks)
                    if with_p:
                        m["p"] = padded(res[n_t:], lo_p, blocks)
            return run

        def wy(i):
            def run():
                a, b = d.get(("g", i)), d.get(("r", i))
                if g_side:
                    sol = _nn(a["t"], a["rhs"])
                    a["u"] = sol[:, :HEAD_A]
                    a["wq"] = jnp.concatenate([sol[:, HEAD_A:], a["qe"]], 0)
                    return
                akv = _nn(b["ak"], b["v10"])
                sol_r = _nn(b["t"], jnp.concatenate([b["a01"], akv], 1))
                b["wr"] = jnp.concatenate([sol_r[:c_len, :LANES] + sol_r[c_len:, :LANES], b["rt"]], 0)
                b["u"] = sol_r[:c_len, LANES:] + sol_r[c_len:, LANES:]
            return run

        def read_state(i):
            def run():
                a, b = d.get(("g", i)), d.get(("r", i))
                if g_side:
                    a["ws"] = _nn(a["wq"], sg[i])
                else:
                    b["ws"] = _nt(b["wr"], sw[i])
            return run

        def update_state(i):
            def run():
                a, b = d.get(("g", i)), d.get(("r", i))
                if g_side:
                    a["v_new"] = a["u"] - a["ws"][:c_len]
                    sg[i] = sg[i] * a["e_last"] + _tn(a["kd"], a["v_new"])
                else:
                    b["pv"] = jnp.concatenate([b["u"] + b["ws"][:c_len], b["v"]], 0)
                    sw[i] = sw[i] * b["w_last"] + jnp.where(same_head, _tn(b["pv"], b["bk_bar"]), 0.0)
            return run

        def outputs(i):
            def run():
                a, b = d.get(("g", i)), d.get(("r", i))
                if g_side:
                    a["o"] = a["ws"][c_len:] + _nn(a["qk"], a["v_new"])
                else:
                    y_ = _nn(b["rbk"], b["pv"])
                    b["y"] = b["ws"][c_len:] + jnp.where(m0, y_[:c_len], y_[c_len:])
            return run

        heads = range(n_heads)
        steps = [(6, gram(i)) for i in heads]
        for k in range(n_sq + 1):
            if n_sq > 0:
                steps += [(3 if k in (0, n_sq) else 6, power_step(i, k, k > 0, k < n_sq)) for i in heads]
        steps += [(5, wy(i)) for i in heads] + [(4, read_state(i)) for i in heads]
        return steps + [(4, update_state(i)) for i in heads] + [(3, outputs(i)) for i in heads]

    def back(c):
        d = st[c]
        rows = slice(c * c_len, (c + 1) * c_len)

        def head(h):
            def run():
                ls = slice(h * LANES, (h + 1) * LANES)
                o = d["g", h]["o"]
                o = o * lax.rsqrt(jnp.mean(o * o, -1, keepdims=True) + NORM_EPS) * ng_ref[...]
                z = z_ref[rows, ls].astype(F32)
                o_a = o * _silu(z)
                y = d["r", h]["y"]
                mu = per_head_sum(y) * (1.0 / nb)
                yc = y - mu
                var = per_head_sum(yc * yc) * (1.0 / nb)
                y = yc * lax.rsqrt(var + GN_EPS) * lng_ref[:, ls] + lnb_ref[:, ls] + d["r", h]["bonus"]
                o_b = y * d["gate"][:, ls]
                o_ref[rows, ls] = (_sigmoid(ga_ref[rows, ls].astype(F32)) * o_a
                                   + _sigmoid(gb_ref[rows, ls].astype(F32)) * o_b).astype(o_ref.dtype)
            return run

        return [(1, head(h)) for h in range(n_heads)]

    def run_merged(*lists):
        order = []
        for k, steps in enumerate(lists):
            total, done = sum(w for w, _ in steps), 0.0
            for i, (w, _) in enumerate(steps):
                order.append(((done + 0.5 * w) / total, k, i))
                done += w
        for _, k, i in sorted(order):
            lists[k][i][1]()

    n_col_tiles = n_cols // proj_tn

    def project_next(c):
        def tile(j):
            return lambda: project(x_next_ref, 1 - slot, j)
        mine = range(c * n_col_tiles // n_chunks, (c + 1) * n_col_tiles // n_chunks)
        return [(1, tile(j)) for j in mine]

    def halves(steps):
        total, done = sum(w for w, _ in steps), 0.0
        for i, (w, _) in enumerate(steps):
            if done >= total / 2:
                return steps[:i], steps[i:]
            done += w
        return steps, []

    g_half = [halves(mxu(c, "g")) for c in range(n_chunks)]
    r_half = [halves(mxu(c, "r")) for c in range(n_chunks)]
    p_half = [halves(project_next(c)) for c in range(n_chunks)]
    run_merged(front(0, "g"))
    for hp in range(2 * n_chunks + 2):
        c, odd = divmod(hp, 2)
        stages = []
        if not odd:
            if c < n_chunks:
                stages += [g_half[c][0], front(c, "r"), p_half[c][0]]
            if c >= 1:
                stages.append(r_half[c - 1][1])
        else:
            if c < n_chunks:
                stages += [g_half[c][1], r_half[c][0], p_half[c][1]]
            if c + 1 < n_chunks:
                stages.append(front(c + 1, "g"))
            if c >= 1:
                stages.append(back(c - 1))
        run_merged(*[s_ for s_ in stages if s_])

    for h in range(n_heads):
        sg_scr[h] = sg[h]
        sw_scr[h] = sw[h]
    for x_ref, hs in ((q_ref, xq_scr), (k_ref, xk_scr), (v_ref, xv_scr), (r_ref, xr_scr), (bk_ref, xbk_scr),
                      (bv_ref, xbv_scr), (l_ref, xl_scr), (g_ref, xg_scr)):
        hs[...] = x_ref[tc - SUB:tc, :].astype(F32)

    @pl.when(t == pl.num_programs(1) - 1)
    def _():
        tail_ref[...] = proj_scr[slot, tc - BF16_ROWS:tc, :]
        sg_out_ref[...] = sg_scr[...]
        for p in range(n_pairs):
            s_p = sw_scr[p]
            sw_out_ref[2 * p] = s_p[:nb, :nb]
            sw_out_ref[2 * p + 1] = s_p[nb:, nb:]


def _mixers(x, conv_state, s_gdn, shift_state, s_wkv, p, tc, chunk):
    b, t_len, d_model = x.shape
    cb = p["cb"]
    w_in = p["w_in"]
    n_cols = w_in.shape[1]
    n_heads = s_gdn.shape[1]
    d = n_heads * HEAD_A
    n_pairs = s_wkv.shape[1] * HEAD_B // LANES
    assert n_pairs == n_heads and d == n_pairs * LANES
    n_t = t_len // tc
    per = d // LANES
    small = 3 * per
    tiles = x.reshape(b * n_t, tc, d_model)
    last_tile = b * n_t - 1

    def ccol(g):
        return pl.BlockSpec((None, CONV_W - 1, d), lambda bi, ti: (bi, 0, g))

    def wcol(g):
        return pl.BlockSpec((CONV_W, d), lambda bi, ti: (0, g))

    def scol(g):
        return pl.BlockSpec((None, 1, d), lambda bi, ti: (bi, 0, g))

    def sfix(blk):
        return pl.BlockSpec((None, 1, LANES), lambda bi, ti: (bi, 0, blk))

    def mcol(g):
        return pl.BlockSpec((1, d), lambda bi, ti: (0, g))

    def mfix(blk):
        return pl.BlockSpec((1, LANES), lambda bi, ti: (0, blk))

    def full(a):
        return pl.BlockSpec(a.shape, lambda bi, ti: (0,) * a.ndim)

    row = pl.BlockSpec((1, LANES), lambda bi, ti: (0, 0))
    st_g = pl.BlockSpec((None,) + s_gdn.shape[1:], lambda bi, ti: (bi, 0, 0, 0))
    st_w = pl.BlockSpec((None,) + s_wkv.shape[1:], lambda bi, ti: (bi, 0, 0, 0))
    prows = [p[n] for n in ("w0", "a0", "k_k", "k_a", "r_k", "lnx_g", "lnx_b", "w_up", "a_up", "g_up")]
    kern = functools.partial(_mixer_kernel, chunk=chunk, n_chunks=tc // chunk, cb=cb, proj_tn=p["proj_tn"])
    wide = pltpu.VMEM((SUB, d), F32)
    narrow = pltpu.VMEM((SUB, LANES), F32)
    x_tile = (None, tc, d_model)
    return pl.pallas_call(
        kern,
        grid=(b, n_t),
        in_specs=[pl.BlockSpec(x_tile, lambda bi, ti: (bi * n_t + ti, 0, 0)),
                  pl.BlockSpec(x_tile, lambda bi, ti: (jnp.minimum(bi * n_t + ti + 1, last_tile), 0, 0)),
                  pl.BlockSpec(w_in.shape, lambda bi, ti: (0, 0), pipeline_mode=pl.Buffered(1)),
                  ccol(0), ccol(1), ccol(2), st_g, wcol(0), wcol(1), wcol(2), row, row, row,
                  scol(0), scol(1), scol(2), sfix(small), sfix(small + 1),
                  mcol(0), mcol(1), mcol(2), mfix(small), mfix(small + 1)]
                 + [full(a) for a in prows] + [st_w],
        out_specs=[pl.BlockSpec((None, tc, d), lambda bi, ti: (bi, ti, 0)), st_g, st_w,
                   pl.BlockSpec((None, BF16_ROWS, n_cols), lambda bi, ti: (bi, 0, 0))],
        out_shape=[jax.ShapeDtypeStruct((b, t_len, d), BF16),
                   jax.ShapeDtypeStruct(s_gdn.shape, F32), jax.ShapeDtypeStruct(s_wkv.shape, F32),
                   jax.ShapeDtypeStruct((b, BF16_ROWS, n_cols), BF16)],
        scratch_shapes=[pltpu.VMEM((2, tc, n_cols), BF16),
                        pltpu.VMEM((n_heads, HEAD_A, HEAD_A), F32), pltpu.VMEM((n_pairs, LANES, LANES), F32),
                        wide, wide, wide, wide, wide, wide, narrow, narrow],
        compiler_params=pltpu.CompilerParams(
            dimension_semantics=("arbitrary", "arbitrary"), vmem_limit_bytes=VMEM_LIMIT),
        name="mixers",
    )(tiles, tiles, w_in, conv_state, conv_state, conv_state, s_gdn,
      p["conv_w"], p["conv_w"], p["conv_w"], p["alog_row"], p["dtb_row"], p["gdn_norm_g"],
      shift_state, shift_state, shift_state, shift_state, shift_state,
      p["mu_b"], p["mu_b"], p["mu_b"], p["mu_b"], p["mu_b"], *prows, s_wkv)


def _mix_ffn_kernel(x_ref, m_ref, wo_ref, g1_ref, b1_ref, w1_ref, w2_ref, g2_ref, b2_ref,
                    y_ref, *, alpha, ff_chunk):
    mix = jnp.dot(m_ref[...], wo_ref[...], preferred_element_type=F32)
    h = _layer_norm(alpha * x_ref[...] + mix, g1_ref[...], b1_ref[...])
    hb = h.astype(BF16)
    d_ff = w1_ref.shape[1]
    ff = jnp.zeros_like(h)
    for c in range(d_ff // ff_chunk):
        cs = slice(c * ff_chunk, (c + 1) * ff_chunk)
        a = jnp.maximum(jnp.dot(hb, w1_ref[:, cs], preferred_element_type=F32), 0.0)
        ff = ff + jnp.dot((a * a).astype(BF16), w2_ref[cs, :], preferred_element_type=F32)
    y_ref[...] = _layer_norm(alpha * h + ff, g2_ref[...], b2_ref[...])


def _mix_ffn(x2d, merged, w_out, ln1_g, ln1_b, w_ff1, w_ff2, ln2_g, ln2_b, alpha, tm):
    n, d = x2d.shape
    d_ff = w_ff1.shape[1]

    def const(shape):
        return pl.BlockSpec(shape, lambda i: (0, 0), pipeline_mode=pl.Buffered(1))

    tile = pl.BlockSpec((tm, d), lambda i: (i, 0))
    kern = functools.partial(_mix_ffn_kernel, alpha=alpha, ff_chunk=min(d_ff, 1024))
    return pl.pallas_call(
        kern,
        grid=(n // tm,),
        in_specs=[tile, tile, const((d, d)), const((1, d)), const((1, d)),
                  const((d, d_ff)), const((d_ff, d)), const((1, d)), const((1, d))],
        out_specs=tile,
        out_shape=jax.ShapeDtypeStruct((n, d), F32),
        compiler_params=pltpu.CompilerParams(
            dimension_semantics=("parallel",), vmem_limit_bytes=VMEM_LIMIT),
        name="mix_ffn",
    )(x2d, merged, w_out, ln1_g, ln1_b, w_ff1, w_ff2, ln2_g, ln2_b)


def _pick_tile(n, target):
    t = min(n, target)
    while n % t:
        t //= 2
    return t


def _layer(x, conv_buf, s_gdn, shift_buf, s_wkv, p):
    b, t_len, d = x.shape
    n = b * t_len
    cb = p["cb"]
    x2d = x.reshape(n, d)

    chunk = min(CHUNK, t_len)
    tc = _pick_tile(t_len, TIME_TILE)
    assert tc >= BF16_ROWS
    merged, s_gdn_new, s_wkv_new, tail = _mixers(x, conv_buf, s_gdn, shift_buf[:, None, :], s_wkv, p, tc, chunk)

    y = _mix_ffn(x2d, merged.reshape(n, -1), p["w_out"], p["ln1_g"], p["ln1_b"],
                 p["w_ff1"], p["w_ff2"], p["ln2_g"], p["ln2_b"], p["alpha"], _pick_tile(n, 512))

    conv_cols = conv_buf.shape[-1]
    hist = CONV_W - 1
    assert t_len >= hist
    conv_new = tail[:, BF16_ROWS - hist:, :conv_cols].astype(x.dtype)
    last = tail[:, BF16_ROWS - 1]
    shift_new = jnp.concatenate([last[:, cb["br"] * LANES:cb["gate_a"] * LANES],
                                 last[:, cb["lora"] * LANES:cb["ba"] * LANES]], -1).astype(x.dtype)
    return y.reshape(b, t_len, d), conv_new, s_gdn_new, shift_new, s_wkv_new


def _prep_layer(l, depth, w_in, conv_a_w, a_log, dt_bias, gdn_norm_g, mu_b, w0, w_up, a0, a_up, g_up,
                k_k, k_a, r_k, lnx_g, lnx_b, w_out, ln1_g, ln1_b, w_ff1, w_ff2, ln2_g, ln2_b):
    n_heads_a = a_log.shape[1]
    conv_ch = conv_a_w.shape[2]
    d_model = w_out.shape[1]
    v_a = n_heads_a * HEAD_A
    cols_b = mu_b.shape[1]
    small0 = conv_ch + v_a
    small1 = small0 + 2 * n_heads_a
    big_b = cols_b - (w_up.shape[1] + a_up.shape[1] + g_up.shape[1])
    w = w_in[l]
    w_p = jnp.concatenate(
        [w[:, :small0], w[:, small1:small1 + big_b], w[:, small1 + cols_b:], w[:, small1 + big_b:small1 + cols_b],
         jnp.pad(w[:, small0:small1], ((0, 0), (0, LANES - 2 * n_heads_a)))], axis=1).astype(BF16)
    blk = lambda c: c // LANES
    cb = {"q": 0, "k": blk(n_heads_a * HEAD_A), "v": blk(2 * n_heads_a * HEAD_A), "z": blk(conv_ch),
          "br": blk(small0)}
    cb["bk"] = cb["br"] + blk(big_b) // 3
    cb["bv"] = cb["bk"] + blk(big_b) // 3
    cb["gate_a"] = cb["br"] + blk(big_b)
    cb["gate_b"] = cb["gate_a"] + blk(d_model)
    cb["lora"] = cb["gate_b"] + blk(d_model)
    cb["gd"] = cb["lora"] + blk(w_up.shape[1] + a_up.shape[1])
    cb["ba"] = cb["gd"] + blk(g_up.shape[1])
    n_blocks = w_p.shape[1] // LANES
    tn_blocks = max(f for f in range(1, PROJ_TILE_BLOCKS + 1) if n_blocks % f == 0)

    def lane_row(vec, offset):
        return jnp.zeros((1, LANES), F32).at[0, offset:offset + vec.shape[0]].set(vec)

    row = lambda a: a[l].reshape(1, -1)
    return {
        "cb": cb, "proj_tn": tn_blocks * LANES, "w_in": w_p, "conv_w": conv_a_w[l],
        "alog_row": lane_row(a_log[l], n_heads_a), "dtb_row": lane_row(dt_bias[l], n_heads_a),
        "gdn_norm_g": row(gdn_norm_g), "mu_b": row(mu_b), "w0": row(w0), "a0": row(a0),
        "k_k": row(k_k), "k_a": row(k_a), "r_k": row(r_k), "lnx_g": row(lnx_g), "lnx_b": row(lnx_b),
        "w_up": w_up[l].astype(BF16), "a_up": a_up[l].astype(BF16), "g_up": g_up[l].astype(BF16),
        "w_out": w_out[l].astype(BF16), "ln1_g": row(ln1_g), "ln1_b": row(ln1_b),
        "w_ff1": w_ff1[l].astype(BF16), "w_ff2": w_ff2[l].astype(BF16),
        "ln2_g": row(ln2_g), "ln2_b": row(ln2_b), "alpha": float((2 * depth) ** 0.25),
    }


def kernel(x_prompt, x_sample, state_conv_a, state_gdn, state_shift_b, state_wkv, w_in, conv_a_w, a_log, dt_bias, gdn_norm_g, mu_b, w0, w_up, a0, a_up, g_up, k_k, k_a, r_k, lnx_g, lnx_b, w_out, ln1_g, ln1_b, w_ff1, w_ff2, ln2_g, ln2_b):
    weights = (w_in, conv_a_w, a_log, dt_bias, gdn_norm_g, mu_b, w0, w_up, a0, a_up, g_up,
               k_k, k_a, r_k, lnx_g, lnx_b, w_out, ln1_g, ln1_b, w_ff1, w_ff2, ln2_g, ln2_b)
    depth = w_in.shape[0]
    bp, dtype = x_prompt.shape[0], x_prompt.dtype
    hp, hs = x_prompt, x_sample
    out_p, out_s = [], []
    for l in range(depth):
        p = _prep_layer(l, depth, *weights)
        zeros = lambda ref: jnp.zeros((bp,) + ref.shape[2:], dtype)
        hp, *st_p = _layer(hp, zeros(state_conv_a), zeros(state_gdn), zeros(state_shift_b),
                           zeros(state_wkv), p)
        hs, *st_s = _layer(hs, state_conv_a[l], state_gdn[l], state_shift_b[l], state_wkv[l], p)
        out_p.append(st_p)
        out_s.append(st_s)
    conv_p, gdn_p, shift_p, wkv_p = (jnp.stack([s[i] for s in out_p]) for i in range(4))
    conv_s, gdn_s, shift_s, wkv_s = (jnp.stack([s[i] for s in out_s]) for i in range(4))
    return (hp, hs, conv_p, gdn_p, shift_p, wkv_p, conv_s, gdn_s, shift_s, wkv_s)
```

```python
import functools
import math

import jax
import jax.numpy as jnp
from jax import lax
from jax.experimental import pallas as pl
from jax.experimental.pallas import tpu as pltpu

F32 = jnp.float32
BF16 = jnp.bfloat16

LANES = 128
MXU_COLS = 256
SUB = 8
BF16_ROWS = 16
HEAD_A = 128
HEAD_B = 64
CONV_W = 4
LN_EPS = 1e-5
NORM_EPS = 1e-6
GN_EPS = 64e-5
DECAY_SCALE = math.exp(-0.5)
CHUNK = 64
TIME_TILE = 256
PROJ_TILE_BLOCKS = 5
VMEM_LIMIT = 56 * 1024 * 1024


def _dot(a, b, dims):
    return lax.dot_general(a.astype(BF16), b.astype(BF16), (dims, ((), ())), preferred_element_type=F32)


def _nn(a, b):
    return _dot(a, b, ((1,), (0,)))


def _nt(a, b):
    return _dot(a, b, ((1,), (1,)))


def _tn(a, b):
    return _dot(a, b, ((0,), (0,)))


def _sigmoid(x):
    return 0.5 * jnp.tanh(0.5 * x) + 0.5


def _silu(x):
    h = 0.5 * x
    return h + h * jnp.tanh(h)


def _softplus(x):
    return jnp.maximum(x, 0.0) + jnp.log1p(jnp.exp(-jnp.abs(x)))


def _layer_norm(x, g, b):
    mu = jnp.mean(x, -1, keepdims=True)
    xc = x - mu
    var = jnp.mean(xc * xc, -1, keepdims=True)
    return xc * lax.rsqrt(var + LN_EPS) * g + b


def _tri_masks(c):
    ii = lax.broadcasted_iota(jnp.int32, (c, c), 0)
    jj = lax.broadcasted_iota(jnp.int32, (c, c), 1)
    return ii >= jj, ii > jj, ii == jj, ii <= jj


def _cumsum_rows(ltri, x):
    n = x.shape[1]
    hi = x.astype(BF16)
    r1 = x - hi.astype(F32)
    mid = r1.astype(BF16)
    lo = (r1 - mid.astype(F32)).astype(BF16)
    res = jnp.dot(ltri, jnp.concatenate([hi, mid, lo], -1), preferred_element_type=F32)
    return res[:, :n] + res[:, n:2 * n] + res[:, 2 * n:]


def _interleave(a, b):
    return [x for pair in zip(a, b) for x in pair]


class _ColumnGroup:
    def __init__(self, buf, slot, first_col, width):
        self.buf, self.slot, self.first_col, self.shape = buf, slot, first_col, (buf.shape[1], width)

    def __getitem__(self, idx):
        rows, cols = idx
        lo = self.first_col + (cols.start or 0)
        hi = self.first_col + (self.shape[1] if cols.stop is None else cols.stop)
        return self.buf[self.slot, rows, lo:hi]


def _mixer_kernel(
        x_cur_ref, x_next_ref, win_ref,
        cq_ref, ck_ref, cv_ref, sg0_ref,
        wq_ref, wk_ref, wv_ref, alog_ref, dtb_ref, ng_ref,
        shr_ref, shk_ref, shv_ref, shl_ref, shg_ref,
        mur_ref, muk_ref, muv_ref, mul_ref, mug_ref,
        w0_ref, a0_ref, kk_ref, ka_ref, rk_ref, lng_ref, lnb_ref,
        wup_ref, aup_ref, gup_ref, sw0_ref,
        o_ref, sg_out_ref, sw_out_ref, tail_ref,
        proj_scr, sg_scr, sw_scr, xq_scr, xk_scr, xv_scr, xr_scr, xbk_scr, xbv_scr, xl_scr, xg_scr,
        *, chunk, n_chunks, cb, proj_tn):
    t = pl.program_id(1)
    step = pl.program_id(0) * pl.num_programs(1) + t
    slot = step % 2
    c_len = chunk
    c2 = 2 * chunk
    tc = chunk * n_chunks
    hist = CONV_W - 1
    nb = HEAD_B
    n_heads = sg_scr.shape[0]
    n_pairs = sw_scr.shape[0]
    per_group = MXU_COLS // LANES
    group_w = per_group * LANES
    d_wide = n_heads * HEAD_A
    n_cols = proj_scr.shape[2]

    def project(x_ref, dst_slot, col_tile):
        cols = slice(col_tile * proj_tn, (col_tile + 1) * proj_tn)
        proj_scr[dst_slot, :, cols] = jnp.dot(x_ref[...].astype(BF16), win_ref[:, cols],
                                              preferred_element_type=F32).astype(BF16)

    @pl.when(step == 0)
    def _():
        for j in range(n_cols // proj_tn):
            project(x_cur_ref, slot, j)

    def group(name, width):
        return _ColumnGroup(proj_scr, slot, cb[name] * LANES, width)

    q_ref, k_ref, v_ref, z_ref, ga_ref = (group(n, d_wide) for n in ("q", "k", "v", "z", "gate_a"))
    r_ref, bk_ref, bv_ref, gb_ref = (group(n, d_wide) for n in ("br", "bk", "bv", "gate_b"))
    l_ref, g_ref, ba_ref = (group(n, LANES) for n in ("lora", "gd", "ba"))

    @pl.when(t == 0)
    def _():
        sg_scr[...] = sg0_ref[...]
        xq_scr[SUB - hist:SUB, :] = cq_ref[...]
        xk_scr[SUB - hist:SUB, :] = ck_ref[...]
        xv_scr[SUB - hist:SUB, :] = cv_ref[...]
        zero = jnp.zeros((nb, nb), F32)
        for p in range(n_pairs):
            sw_scr[p] = jnp.concatenate([jnp.concatenate([sw0_ref[2 * p], zero], 1),
                                         jnp.concatenate([zero, sw0_ref[2 * p + 1]], 1)], 0)
        xr_scr[SUB - 1:SUB, :] = shr_ref[...]
        xbk_scr[SUB - 1:SUB, :] = shk_ref[...]
        xbv_scr[SUB - 1:SUB, :] = shv_ref[...]
        xl_scr[SUB - 1:SUB, :] = shl_ref[...]
        xg_scr[SUB - 1:SUB, :] = shg_ref[...]

    def shifted(x_ref, hs, c, n_shifts, cols, x):
        if c == 0:
            prev = hs[:, cols]
        else:
            lo = c * c_len - BF16_ROWS
            prev = x_ref[lo:lo + BF16_ROWS, cols].astype(F32)[BF16_ROWS - SUB:]
        xw = jnp.concatenate([prev, x], 0)
        return [xw[SUB - 1 - j:SUB - 1 - j + c_len] for j in range(n_shifts)]

    lane = lax.broadcasted_iota(jnp.int32, (1, LANES), 1)
    m0 = lane < nb
    lora_w = wup_ref.shape[0]

    def per_head_sum(x):
        s_lo = jnp.sum(jnp.where(m0, x, 0.0), -1, keepdims=True)
        s_hi = jnp.sum(jnp.where(m0, 0.0, x), -1, keepdims=True)
        return jnp.where(m0, s_lo, s_hi)

    incl, strict, eye_m, upper = _tri_masks(c_len)
    eye = eye_m.astype(F32)
    ltri = incl.astype(BF16)
    n_sq = max(c_len.bit_length() - 2, 0)
    ra = lax.broadcasted_iota(jnp.int32, (3 * c_len, c2), 0)
    ca = lax.broadcasted_iota(jnp.int32, (3 * c_len, c2), 1)
    ca_t = jnp.where(ca >= c_len, ca - c_len, ca)
    ra_t = jnp.where(ra < c_len, ra - 1, (ra - c_len) % c_len)
    mask_a = ra_t >= ca_t
    rb_i = lax.broadcasted_iota(jnp.int32, (c_len, c2), 0)
    cb_i = lax.broadcasted_iota(jnp.int32, (c_len, c2), 1)
    first_cols = cb_i < c_len
    mask_b = rb_i > jnp.where(first_cols, cb_i, cb_i - c_len)
    ii2 = lax.broadcasted_iota(jnp.int32, (c2, c2), 0)
    jj2 = lax.broadcasted_iota(jnp.int32, (c2, c2), 1)
    eye2 = (ii2 == jj2).astype(F32)
    si = lax.broadcasted_iota(jnp.int32, (LANES, LANES), 0)
    sj = lax.broadcasted_iota(jnp.int32, (LANES, LANES), 1)
    same_head = (si // nb) == (sj // nb)

    st = [dict() for _ in range(n_chunks)]
    sg = [sg_scr[h] for h in range(n_heads)]
    sw = [sw_scr[p] for p in range(n_pairs)]

    def front(c, part):
        d = st[c]
        rows = slice(c * c_len, (c + 1) * c_len)

        def conv_step(name, x_ref, hs, w_ref, grp):
            def run():
                cols = slice(grp * group_w, (grp + 1) * group_w)
                w = w_ref[:, cols]
                x = x_ref[rows, cols].astype(F32)
                acc = x * w[hist:CONV_W, :]
                for j, prev in enumerate(shifted(x_ref, hs, c, hist, cols, x)):
                    acc = acc + prev * w[hist - 1 - j:hist - j, :]
                d[name, grp] = _silu(acc)
            return run

        def mix_step(name, x_ref, hs, mu_ref, grp):
            def run():
                cols = slice(grp * group_w, (grp + 1) * group_w) if x_ref.shape[1] > group_w else slice(None)
                x = x_ref[rows, cols].astype(F32)
                d[name, grp] = x + mu_ref[:, cols] * (shifted(x_ref, hs, c, 1, cols, x)[0] - x)
            return run

        def wide(name, h):
            off = h % per_group * LANES
            return d[name, h // per_group][:, off:off + LANES]

        def gates_g():
            ba = ba_ref[rows, :].astype(F32)
            d["beta_all"] = _sigmoid(ba)
            d["g_all"] = -jnp.exp(alog_ref[...]) * _softplus(ba + dtb_ref[...])

        def gates_r():
            xl = d["xl", 0]
            u = w0_ref[...] + _nn(jnp.tanh(xl[:, :lora_w]), wup_ref[...])
            d["logw"] = -DECAY_SCALE * _sigmoid(u)
            d["a_all"] = _sigmoid(a0_ref[...] + _nn(xl[:, lora_w:], aup_ref[...]))
            d["gate"] = _nn(_sigmoid(d["xg", 0]), gup_ref[...])

        def gdn_prep(h):
            def run():
                ls = slice(h * HEAD_A, (h + 1) * HEAD_A)
                qh, kh, v = wide("qc", h), wide("kc", h), wide("vc", h)
                q = qh * lax.rsqrt(jnp.sum(qh * qh, -1, keepdims=True) + NORM_EPS) * (HEAD_A ** -0.5)
                k = kh * lax.rsqrt(jnp.sum(kh * kh, -1, keepdims=True) + NORM_EPS)
                beta = jnp.sum(jnp.where(lane == h, d["beta_all"], 0.0), -1, keepdims=True)
                g = jnp.sum(jnp.where(lane == h + n_heads, d["g_all"], 0.0), -1, keepdims=True)
                g_row = jnp.sum(jnp.where(eye_m, g, 0.0), 0, keepdims=True)
                gc = jnp.sum(jnp.where(incl, g_row, 0.0), 1, keepdims=True)
                gc_row = jnp.sum(jnp.where(upper, g, 0.0), 0, keepdims=True)
                eg = jnp.exp(gc)
                g_last = gc[c_len - 1:c_len, :]
                kb = k * beta
                d["g", h] = {
                    "decay": jnp.where(incl, jnp.exp(jnp.where(incl, gc - gc_row, 0.0)), 0.0),
                    "k": k, "kbq": jnp.concatenate([kb, q], 0),
                    "rhs": jnp.concatenate([v * beta, kb * eg], -1), "qe": q * eg,
                    "kd": k * jnp.exp(g_last - gc), "e_last": jnp.exp(g_last)}
            return run

        def rwkv_prep(p):
            def run():
                ls = slice(p * LANES, (p + 1) * LANES)
                r_p, k_p, v_p, a_p = wide("xr", p), wide("xk", p), wide("xv", p), d["a_all"][:, ls]
                kkv = k_p * kk_ref[:, ls]
                kk = kkv * lax.rsqrt(per_head_sum(kkv * kkv) + NORM_EPS)
                k2 = k_p * (1.0 + (a_p - 1.0) * ka_ref[:, ls])
                bvec = kk * a_p
                lw = d["logw"][:, ls]
                cl = _cumsum_rows(ltri, lw)
                cl_last = cl[c_len - 1:c_len, :]
                e_neg = jnp.exp(-cl)
                rt = r_p * jnp.exp(cl)
                kt = k2 * e_neg
                bt = bvec * e_neg
                at = -kk * jnp.exp(cl - lw)
                e_tail = jnp.exp(cl_last - cl)
                at0 = jnp.where(m0, at, 0.0)
                at1 = jnp.where(m0, 0.0, at)
                d["r", p] = {
                    "bonus": per_head_sum(r_p * k2 * rk_ref[:, ls]) * v_p, "v": v_p, "rt": rt,
                    "lhs_a": jnp.concatenate([at0, jnp.where(m0, rt, 0.0), jnp.where(m0, 0.0, rt)], 0),
                    "rhs_a": jnp.concatenate([bt, kt], 0), "at1": at1, "rhs_b": jnp.concatenate([kt, bt], 0),
                    "a01": jnp.concatenate([at0, at1], 0),
                    "v10": jnp.concatenate([jnp.where(m0, 0.0, v_p), jnp.where(m0, v_p, 0.0)], 0),
                    "bk_bar": jnp.concatenate([bvec * e_tail, k2 * e_tail], 0), "w_last": jnp.exp(cl_last)}
            return run

        if part == "g":
            steps = [(1, gates_g)]
            for grp in range(n_heads // per_group):
                steps += [(2, conv_step("qc", q_ref, xq_scr, wq_ref, grp)),
                          (2, conv_step("kc", k_ref, xk_scr, wk_ref, grp)),
                          (2, conv_step("vc", v_ref, xv_scr, wv_ref, grp))]
                steps += [(3, gdn_prep(h)) for h in range(grp * per_group, (grp + 1) * per_group)]
            return steps
        steps = [(1, mix_step("xl", l_ref, xl_scr, mul_ref, 0)), (1, mix_step("xg", g_ref, xg_scr, mug_ref, 0)),
                 (7, gates_r)]
        for grp in range(n_heads // per_group):
            steps += [(1, mix_step("xr", r_ref, xr_scr, mur_ref, grp)), (1, mix_step("xk", bk_ref, xbk_scr, muk_ref, grp)),
                      (1, mix_step("xv", bv_ref, xbv_scr, muv_ref, grp))]
            steps += [(4, rwkv_prep(h)) for h in range(grp * per_group, (grp + 1) * per_group)]
        return steps

    def mxu(c, part):
        d = st[c]
        g_side = part == "g"

        def gram(i):
            def run():
                a, b = d.get(("g", i)), d.get(("r", i))
                if g_side:
                    gm = _nt(a["kbq"], a["k"])
                    a["qk"] = gm[c_len:] * a["decay"]
                    a["p"] = jnp.where(strict, -gm[:c_len] * a["decay"], 0.0)
                    a["t"] = eye + a["p"]
                    return
                g_a = jnp.where(mask_a, _nt(b["lhs_a"], b["rhs_a"]), 0.0)
                g_b = jnp.where(mask_b, _nt(b["at1"], b["rhs_b"]), 0.0)
                top = g_a[:c_len]
                b["rbk"] = g_a[c_len:]
                b["p"] = jnp.concatenate([jnp.where(first_cols, top, 0.0), jnp.where(first_cols, 0.0, g_b)], 0)
                b["t"] = eye2 + b["p"]
                b["ak"] = jnp.concatenate([jnp.where(first_cols, 0.0, top), jnp.where(first_cols, g_b, 0.0)], 0)
            return run

        def live_rows(m, lo, blocks):
            return m if lo == 0 else jnp.concatenate([m[b * c_len + lo:(b + 1) * c_len] for b in range(blocks)], 0)

        def zero_rows(k):
            return min(2 ** k // BF16_ROWS * BF16_ROWS, c_len)

        def padded(res, lo, blocks):
            if lo == 0:
                return res
            zero = jnp.zeros((lo, res.shape[1]), F32)
            n = c_len - lo
            return jnp.concatenate([x for b in range(blocks) for x in (zero, res[b * n:(b + 1) * n])], 0)

        def added(t_, add, lo, blocks):
            if lo == 0:
                return t_ + add
            n = c_len - lo
            return jnp.concatenate([x for b in range(blocks)
                                    for x in (t_[b * c_len:b * c_len + lo],
                                              t_[b * c_len + lo:(b + 1) * c_len] + add[b * n:(b + 1) * n])], 0)

        def power_step(i, k, with_t, with_p):
            def run():
                for m, blocks in (((d["g", i], 1),) if g_side else ((d["r", i], 2),)):
                    lo_t, lo_p = zero_rows(k), zero_rows(k + 1)
                    lhs = ([live_rows(m["t"], lo_t, blocks)] if with_t else []) \
                        + ([live_rows(m["p"], lo_p, blocks)] if with_p else [])
                    res = _nn(lhs[0] if len(lhs) == 1 else jnp.concatenate(lhs, 0), m["p"])
                    n_t = blocks * (c_len - lo_t) if with_t else 0
                    if with_t:
                        m["t"] = added(m["t"], res[:n_t], lo_t, blocks)
                    if with_p:
                        m["p"] = padded(res[n_t:], lo_p, blocks)
            return run

        def wy(i):
            def run():
                a, b = d.get(("g", i)), d.get(("r", i))
                if g_side:
                    sol = _nn(a["t"], a["rhs"])
                    a["u"] = sol[:, :HEAD_A]
                    a["wq"] = jnp.concatenate([sol[:, HEAD_A:], a["qe"]], 0)
                    return
                akv = _nn(b["ak"], b["v10"])
                sol_r = _nn(b["t"], jnp.concatenate([b["a01"], akv], 1))
                b["wr"] = jnp.concatenate([sol_r[:c_len, :LANES] + sol_r[c_len:, :LANES], b["rt"]], 0)
                b["u"] = sol_r[:c_len, LANES:] + sol_r[c_len:, LANES:]
            return run

        def read_state(i):
            def run():
                a, b = d.get(("g", i)), d.get(("r", i))
                if g_side:
                    a["ws"] = _nn(a["wq"], sg[i])
                else:
                    b["ws"] = _nt(b["wr"], sw[i])
            return run

        def update_state(i):
            def run():
                a, b = d.get(("g", i)), d.get(("r", i))
                if g_side:
                    a["v_new"] = a["u"] - a["ws"][:c_len]
                    sg[i] = sg[i] * a["e_last"] + _tn(a["kd"], a["v_new"])
                else:
                    b["pv"] = jnp.concatenate([b["u"] + b["ws"][:c_len], b["v"]], 0)
                    sw[i] = sw[i] * b["w_last"] + jnp.where(same_head, _tn(b["pv"], b["bk_bar"]), 0.0)
            return run

        def outputs(i):
            def run():
                a, b = d.get(("g", i)), d.get(("r", i))
                if g_side:
                    a["o"] = a["ws"][c_len:] + _nn(a["qk"], a["v_new"])
                else:
                    y_ = _nn(b["rbk"], b["pv"])
                    b["y"] = b["ws"][c_len:] + jnp.where(m0, y_[:c_len], y_[c_len:])
            return run

        heads = range(n_heads)
        steps = [(6, gram(i)) for i in heads]
        for k in range(n_sq + 1):
            if n_sq > 0:
                steps += [(3 if k in (0, n_sq) else 6, power_step(i, k, k > 0, k < n_sq)) for i in heads]
        steps += [(5, wy(i)) for i in heads] + [(4, read_state(i)) for i in heads]
        return steps + [(4, update_state(i)) for i in heads] + [(3, outputs(i)) for i in heads]

    def back(c):
        d = st[c]
        rows = slice(c * c_len, (c + 1) * c_len)

        def head(h):
            def run():
                ls = slice(h * LANES, (h + 1) * LANES)
                o = d["g", h]["o"]
                o = o * lax.rsqrt(jnp.mean(o * o, -1, keepdims=True) + NORM_EPS) * ng_ref[...]
                z = z_ref[rows, ls].astype(F32)
                o_a = o * _silu(z)
                y = d["r", h]["y"]
                mu = per_head_sum(y) * (1.0 / nb)
                yc = y - mu
                var = per_head_sum(yc * yc) * (1.0 / nb)
                y = yc * lax.rsqrt(var + GN_EPS) * lng_ref[:, ls] + lnb_ref[:, ls] + d["r", h]["bonus"]
                o_b = y * d["gate"][:, ls]
                o_ref[rows, ls] = (_sigmoid(ga_ref[rows, ls].astype(F32)) * o_a
                                   + _sigmoid(gb_ref[rows, ls].astype(F32)) * o_b).astype(o_ref.dtype)
            return run

        return [(1, head(h)) for h in range(n_heads)]

    def run_merged(*lists):
        order = []
        for k, steps in enumerate(lists):
            total, done = sum(w for w, _ in steps), 0.0
            for i, (w, _) in enumerate(steps):
                order.append(((done + 0.5 * w) / total, k, i))
                done += w
        for _, k, i in sorted(order):
            lists[k][i][1]()

    n_col_tiles = n_cols // proj_tn

    def project_next(c):
        def tile(j):
            return lambda: project(x_next_ref, 1 - slot, j)
        mine = range(c * n_col_tiles // n_chunks, (c + 1) * n_col_tiles // n_chunks)
        return [(1, tile(j)) for j in mine]

    def halves(steps):
        total, done = sum(w for w, _ in steps), 0.0
        for i, (w, _) in enumerate(steps):
            if done >= total / 2:
                return steps[:i], steps[i:]
            done += w
        return steps, []

    g_half = [halves(mxu(c, "g")) for c in range(n_chunks)]
    r_half = [halves(mxu(c, "r")) for c in range(n_chunks)]
    p_half = [halves(project_next(c)) for c in range(n_chunks)]
    run_merged(front(0, "g"))
    for hp in range(2 * n_chunks + 2):
        c, odd = divmod(hp, 2)
        stages = []
        if not odd:
            if c < n_chunks:
                stages += [g_half[c][0], front(c, "r"), p_half[c][0]]
            if c >= 1:
                stages.append(r_half[c - 1][1])
        else:
            if c < n_chunks:
                stages += [g_half[c][1], r_half[c][0], p_half[c][1]]
            if c + 1 < n_chunks:
                stages.append(front(c + 1, "g"))
            if c >= 1:
                stages.append(back(c - 1))
        run_merged(*[s_ for s_ in stages if s_])

    for h in range(n_heads):
        sg_scr[h] = sg[h]
        sw_scr[h] = sw[h]
    for x_ref, hs in ((q_ref, xq_scr), (k_ref, xk_scr), (v_ref, xv_scr), (r_ref, xr_scr), (bk_ref, xbk_scr),
                      (bv_ref, xbv_scr), (l_ref, xl_scr), (g_ref, xg_scr)):
        hs[...] = x_ref[tc - SUB:tc, :].astype(F32)

    @pl.when(t == pl.num_programs(1) - 1)
    def _():
        tail_ref[...] = proj_scr[slot, tc - BF16_ROWS:tc, :]
        sg_out_ref[...] = sg_scr[...]
        for p in range(n_pairs):
            s_p = sw_scr[p]
            sw_out_ref[2 * p] = s_p[:nb, :nb]
            sw_out_ref[2 * p + 1] = s_p[nb:, nb:]


def _mixers(x, conv_state, s_gdn, shift_state, s_wkv, p, tc, chunk):
    b, t_len, d_model = x.shape
    cb = p["cb"]
    w_in = p["w_in"]
    n_cols = w_in.shape[1]
    n_heads = s_gdn.shape[1]
    d = n_heads * HEAD_A
    n_pairs = s_wkv.shape[1] * HEAD_B // LANES
    assert n_pairs == n_heads and d == n_pairs * LANES
    n_t = t_len // tc
    per = d // LANES
    small = 3 * per
    tiles = x.reshape(b * n_t, tc, d_model)
    last_tile = b * n_t - 1

    def ccol(g):
        return pl.BlockSpec((None, CONV_W - 1, d), lambda bi, ti: (bi, 0, g))

    def wcol(g):
        return pl.BlockSpec((CONV_W, d), lambda bi, ti: (0, g))

    def scol(g):
        return pl.BlockSpec((None, 1, d), lambda bi, ti: (bi, 0, g))

    def sfix(blk):
        return pl.BlockSpec((None, 1, LANES), lambda bi, ti: (bi, 0, blk))

    def mcol(g):
        return pl.BlockSpec((1, d), lambda bi, ti: (0, g))

    def mfix(blk):
        return pl.BlockSpec((1, LANES), lambda bi, ti: (0, blk))

    def full(a):
        return pl.BlockSpec(a.shape, lambda bi, ti: (0,) * a.ndim)

    row = pl.BlockSpec((1, LANES), lambda bi, ti: (0, 0))
    st_g = pl.BlockSpec((None,) + s_gdn.shape[1:], lambda bi, ti: (bi, 0, 0, 0))
    st_w = pl.BlockSpec((None,) + s_wkv.shape[1:], lambda bi, ti: (bi, 0, 0, 0))
    prows = [p[n] for n in ("w0", "a0", "k_k", "k_a", "r_k", "lnx_g", "lnx_b", "w_up", "a_up", "g_up")]
    kern = functools.partial(_mixer_kernel, chunk=chunk, n_chunks=tc // chunk, cb=cb, proj_tn=p["proj_tn"])
    wide = pltpu.VMEM((SUB, d), F32)
    narrow = pltpu.VMEM((SUB, LANES), F32)
    x_tile = (None, tc, d_model)
    return pl.pallas_call(
        kern,
        grid=(b, n_t),
        in_specs=[pl.BlockSpec(x_tile, lambda bi, ti: (bi * n_t + ti, 0, 0)),
                  pl.BlockSpec(x_tile, lambda bi, ti: (jnp.minimum(bi * n_t + ti + 1, last_tile), 0, 0)),
                  pl.BlockSpec(w_in.shape, lambda bi, ti: (0, 0), pipeline_mode=pl.Buffered(1)),
                  ccol(0), ccol(1), ccol(2), st_g, wcol(0), wcol(1), wcol(2), row, row, row,
                  scol(0), scol(1), scol(2), sfix(small), sfix(small + 1),
                  mcol(0), mcol(1), mcol(2), mfix(small), mfix(small + 1)]
                 + [full(a) for a in prows] + [st_w],
        out_specs=[pl.BlockSpec((None, tc, d), lambda bi, ti: (bi, ti, 0)), st_g, st_w,
                   pl.BlockSpec((None, BF16_ROWS, n_cols), lambda bi, ti: (bi, 0, 0))],
        out_shape=[jax.ShapeDtypeStruct((b, t_len, d), BF16),
                   jax.ShapeDtypeStruct(s_gdn.shape, F32), jax.ShapeDtypeStruct(s_wkv.shape, F32),
                   jax.ShapeDtypeStruct((b, BF16_ROWS, n_cols), BF16)],
        scratch_shapes=[pltpu.VMEM((2, tc, n_cols), BF16),
                        pltpu.VMEM((n_heads, HEAD_A, HEAD_A), F32), pltpu.VMEM((n_pairs, LANES, LANES), F32),
                        wide, wide, wide, wide, wide, wide, narrow, narrow],
        compiler_params=pltpu.CompilerParams(
            dimension_semantics=("arbitrary", "arbitrary"), vmem_limit_bytes=VMEM_LIMIT),
        name="mixers",
    )(tiles, tiles, w_in, conv_state, conv_state, conv_state, s_gdn,
      p["conv_w"], p["conv_w"], p["conv_w"], p["alog_row"], p["dtb_row"], p["gdn_norm_g"],
      shift_state, shift_state, shift_state, shift_state, shift_state,
      p["mu_b"], p["mu_b"], p["mu_b"], p["mu_b"], p["mu_b"], *prows, s_wkv)


def _mix_ffn_kernel(x_ref, m_ref, wo_ref, g1_ref, b1_ref, w1_ref, w2_ref, g2_ref, b2_ref,
                    y_ref, *, alpha, ff_chunk):
    mix = jnp.dot(m_ref[...], wo_ref[...], preferred_element_type=F32)
    h = _layer_norm(alpha * x_ref[...] + mix, g1_ref[...], b1_ref[...])
    hb = h.astype(BF16)
    d_ff = w1_ref.shape[1]
    ff = jnp.zeros_like(h)
    for c in range(d_ff // ff_chunk):
        cs = slice(c * ff_chunk, (c + 1) * ff_chunk)
        a = jnp.maximum(jnp.dot(hb, w1_ref[:, cs], preferred_element_type=F32), 0.0)
        ff = ff + jnp.dot((a * a).astype(BF16), w2_ref[cs, :], preferred_element_type=F32)
    y_ref[...] = _layer_norm(alpha * h + ff, g2_ref[...], b2_ref[...])


def _mix_ffn(x2d, merged, w_out, ln1_g, ln1_b, w_ff1, w_ff2, ln2_g, ln2_b, alpha, tm):
    n, d = x2d.shape
    d_ff = w_ff1.shape[1]

    def const(shape):
        return pl.BlockSpec(shape, lambda i: (0, 0), pipeline_mode=pl.Buffered(1))

    tile = pl.BlockSpec((tm, d), lambda i: (i, 0))
    kern = functools.partial(_mix_ffn_kernel, alpha=alpha, ff_chunk=min(d_ff, 1024))
    return pl.pallas_call(
        kern,
        grid=(n // tm,),
        in_specs=[tile, tile, const((d, d)), const((1, d)), const((1, d)),
                  const((d, d_ff)), const((d_ff, d)), const((1, d)), const((1, d))],
        out_specs=tile,
        out_shape=jax.ShapeDtypeStruct((n, d), F32),
        compiler_params=pltpu.CompilerParams(
            dimension_semantics=("parallel",), vmem_limit_bytes=VMEM_LIMIT),
        name="mix_ffn",
    )(x2d, merged, w_out, ln1_g, ln1_b, w_ff1, w_ff2, ln2_g, ln2_b)


def _pick_tile(n, target):
    t = min(n, target)
    while n % t:
        t //= 2
    return t


def _layer(x, conv_buf, s_gdn, shift_buf, s_wkv, p):
    b, t_len, d = x.shape
    n = b * t_len
    cb = p["cb"]
    x2d = x.reshape(n, d)

    chunk = min(CHUNK, t_len)
    tc = _pick_tile(t_len, TIME_TILE)
    assert tc >= BF16_ROWS
    merged, s_gdn_new, s_wkv_new, tail = _mixers(x, conv_buf, s_gdn, shift_buf[:, None, :], s_wkv, p, tc, chunk)

    y = _mix_ffn(x2d, merged.reshape(n, -1), p["w_out"], p["ln1_g"], p["ln1_b"],
                 p["w_ff1"], p["w_ff2"], p["ln2_g"], p["ln2_b"], p["alpha"], _pick_tile(n, 512))

    conv_cols = conv_buf.shape[-1]
    hist = CONV_W - 1
    assert t_len >= hist
    conv_new = tail[:, BF16_ROWS - hist:, :conv_cols].astype(x.dtype)
    last = tail[:, BF16_ROWS - 1]
    shift_new = jnp.concatenate([last[:, cb["br"] * LANES:cb["gate_a"] * LANES],
                                 last[:, cb["lora"] * LANES:cb["ba"] * LANES]], -1).astype(x.dtype)
    return y.reshape(b, t_len, d), conv_new, s_gdn_new, shift_new, s_wkv_new


def _prep_layer(l, depth, w_in, conv_a_w, a_log, dt_bias, gdn_norm_g, mu_b, w0, w_up, a0, a_up, g_up,
                k_k, k_a, r_k, lnx_g, lnx_b, w_out, ln1_g, ln1_b, w_ff1, w_ff2, ln2_g, ln2_b):
    n_heads_a = a_log.shape[1]
    conv_ch = conv_a_w.shape[2]
    d_model = w_out.shape[1]
    v_a = n_heads_a * HEAD_A
    cols_b = mu_b.shape[1]
    small0 = conv_ch + v_a
    small1 = small0 + 2 * n_heads_a
    big_b = cols_b - (w_up.shape[1] + a_up.shape[1] + g_up.shape[1])
    w = w_in[l]
    w_p = jnp.concatenate(
        [w[:, :small0], w[:, small1:small1 + big_b], w[:, small1 + cols_b:], w[:, small1 + big_b:small1 + cols_b],
         jnp.pad(w[:, small0:small1], ((0, 0), (0, LANES - 2 * n_heads_a)))], axis=1).astype(BF16)
    w_p = jnp.pad(w_p, ((0, 0), (0, -w_p.shape[1] % MXU_COLS)))
    blk = lambda c: c // LANES
    cb = {"q": 0, "k": blk(n_heads_a * HEAD_A), "v": blk(2 * n_heads_a * HEAD_A), "z": blk(conv_ch),
          "br": blk(small0)}
    cb["bk"] = cb["br"] + blk(big_b) // 3
    cb["bv"] = cb["bk"] + blk(big_b) // 3
    cb["gate_a"] = cb["br"] + blk(big_b)
    cb["gate_b"] = cb["gate_a"] + blk(d_model)
    cb["lora"] = cb["gate_b"] + blk(d_model)
    cb["gd"] = cb["lora"] + blk(w_up.shape[1] + a_up.shape[1])
    cb["ba"] = cb["gd"] + blk(g_up.shape[1])
    n_blocks = w_p.shape[1] // LANES
    per_mxu = MXU_COLS // LANES
    tn_blocks = max(f for f in range(per_mxu, PROJ_TILE_BLOCKS + 1, per_mxu) if n_blocks % f == 0)

    def lane_row(vec, offset):
        return jnp.zeros((1, LANES), F32).at[0, offset:offset + vec.shape[0]].set(vec)

    row = lambda a: a[l].reshape(1, -1)
    return {
        "cb": cb, "proj_tn": tn_blocks * LANES, "w_in": w_p, "conv_w": conv_a_w[l],
        "alog_row": lane_row(a_log[l], n_heads_a), "dtb_row": lane_row(dt_bias[l], n_heads_a),
        "gdn_norm_g": row(gdn_norm_g), "mu_b": row(mu_b), "w0": row(w0), "a0": row(a0),
        "k_k": row(k_k), "k_a": row(k_a), "r_k": row(r_k), "lnx_g": row(lnx_g), "lnx_b": row(lnx_b),
        "w_up": w_up[l].astype(BF16), "a_up": a_up[l].astype(BF16), "g_up": g_up[l].astype(BF16),
        "w_out": w_out[l].astype(BF16), "ln1_g": row(ln1_g), "ln1_b": row(ln1_b),
        "w_ff1": w_ff1[l].astype(BF16), "w_ff2": w_ff2[l].astype(BF16),
        "ln2_g": row(ln2_g), "ln2_b": row(ln2_b), "alpha": float((2 * depth) ** 0.25),
    }


def kernel(x_prompt, x_sample, state_conv_a, state_gdn, state_shift_b, state_wkv, w_in, conv_a_w, a_log, dt_bias, gdn_norm_g, mu_b, w0, w_up, a0, a_up, g_up, k_k, k_a, r_k, lnx_g, lnx_b, w_out, ln1_g, ln1_b, w_ff1, w_ff2, ln2_g, ln2_b):
    weights = (w_in, conv_a_w, a_log, dt_bias, gdn_norm_g, mu_b, w0, w_up, a0, a_up, g_up,
               k_k, k_a, r_k, lnx_g, lnx_b, w_out, ln1_g, ln1_b, w_ff1, w_ff2, ln2_g, ln2_b)
    depth = w_in.shape[0]
    bp, dtype = x_prompt.shape[0], x_prompt.dtype
    hp, hs = x_prompt, x_sample
    out_p, out_s = [], []
    for l in range(depth):
        p = _prep_layer(l, depth, *weights)
        zeros = lambda ref: jnp.zeros((bp,) + ref.shape[2:], dtype)
        hp, *st_p = _layer(hp, zeros(state_conv_a), zeros(state_gdn), zeros(state_shift_b),
                           zeros(state_wkv), p)
        hs, *st_s = _layer(hs, state_conv_a[l], state_gdn[l], state_shift_b[l], state_wkv[l], p)
        out_p.append(st_p)
        out_s.append(st_s)
    conv_p, gdn_p, shift_p, wkv_p = (jnp.stack([s[i] for s in out_p]) for i in range(4))
    conv_s, gdn_s, shift_s, wkv_s = (jnp.stack([s[i] for s in out_s]) for i in range(4))
    return (hp, hs, conv_p, gdn_p, shift_p, wkv_p, conv_s, gdn_s, shift_s, wkv_s)
```

```python
import functools
import math

import jax
import jax.numpy as jnp
from jax import lax
from jax.experimental import pallas as pl
from jax.experimental.pallas import tpu as pltpu

F32 = jnp.float32
BF16 = jnp.bfloat16

LANES = 128
MXU_COLS = 256
SUB = 8
BF16_ROWS = 16
HEAD_A = 128
HEAD_B = 64
CONV_W = 4
LN_EPS = 1e-5
NORM_EPS = 1e-6
GN_EPS = 64e-5
DECAY_SCALE = math.exp(-0.5)
CHUNK = 64
TIME_TILE = 256
PROJ_TILE_BLOCKS = 5
VMEM_LIMIT = 56 * 1024 * 1024


def _dot(a, b, dims):
    return lax.dot_general(a.astype(BF16), b.astype(BF16), (dims, ((), ())), preferred_element_type=F32)


def _nn(a, b):
    return _dot(a, b, ((1,), (0,)))


def _nt(a, b):
    return _dot(a, b, ((1,), (1,)))


def _tn(a, b):
    return _dot(a, b, ((0,), (0,)))


def _sigmoid(x):
    return 0.5 * jnp.tanh(0.5 * x) + 0.5


def _silu(x):
    h = 0.5 * x
    return h + h * jnp.tanh(h)


def _softplus(x):
    return jnp.maximum(x, 0.0) + jnp.log1p(jnp.exp(-jnp.abs(x)))


def _layer_norm(x, g, b):
    mu = jnp.mean(x, -1, keepdims=True)
    xc = x - mu
    var = jnp.mean(xc * xc, -1, keepdims=True)
    return xc * lax.rsqrt(var + LN_EPS) * g + b


def _tri_masks(c):
    ii = lax.broadcasted_iota(jnp.int32, (c, c), 0)
    jj = lax.broadcasted_iota(jnp.int32, (c, c), 1)
    return ii >= jj, ii > jj, ii == jj, ii <= jj


def _cumsum_rows(ltri, x):
    n = x.shape[1]
    hi = x.astype(BF16)
    lo = (x - hi.astype(F32)).astype(BF16)
    res = jnp.dot(ltri, jnp.concatenate([hi, lo], -1), preferred_element_type=F32)
    return res[:, :n] + res[:, n:]


def _interleave(a, b):
    return [x for pair in zip(a, b) for x in pair]


class _ColumnGroup:
    def __init__(self, buf, slot, first_col, width):
        self.buf, self.slot, self.first_col, self.shape = buf, slot, first_col, (buf.shape[1], width)

    def __getitem__(self, idx):
        rows, cols = idx
        lo = self.first_col + (cols.start or 0)
        hi = self.first_col + (self.shape[1] if cols.stop is None else cols.stop)
        return self.buf[self.slot, rows, lo:hi]


def _mixer_kernel(
        x_cur_ref, x_next_ref, win_ref,
        cq_ref, ck_ref, cv_ref, sg0_ref,
        wq_ref, wk_ref, wv_ref, alog_ref, dtb_ref, ng_ref,
        shr_ref, shk_ref, shv_ref, shl_ref, shg_ref,
        mur_ref, muk_ref, muv_ref, mul_ref, mug_ref,
        w0_ref, a0_ref, kk_ref, ka_ref, rk_ref, lng_ref, lnb_ref,
        wup_ref, aup_ref, gup_ref, sw0_ref,
        o_ref, sg_out_ref, sw_out_ref, tail_ref,
        proj_scr, sg_scr, sw_scr, xq_scr, xk_scr, xv_scr, xr_scr, xbk_scr, xbv_scr, xl_scr, xg_scr,
        *, chunk, n_chunks, cb, proj_tn):
    t = pl.program_id(1)
    step = pl.program_id(0) * pl.num_programs(1) + t
    slot = step % 2
    c_len = chunk
    c2 = 2 * chunk
    tc = chunk * n_chunks
    hist = CONV_W - 1
    nb = HEAD_B
    n_heads = sg_scr.shape[0]
    n_pairs = sw_scr.shape[0]
    per_group = MXU_COLS // LANES
    group_w = per_group * LANES
    d_wide = n_heads * HEAD_A
    n_cols = proj_scr.shape[2]

    def project(x_ref, dst_slot, col_tile):
        cols = slice(col_tile * proj_tn, (col_tile + 1) * proj_tn)
        proj_scr[dst_slot, :, cols] = jnp.dot(x_ref[...].astype(BF16), win_ref[:, cols],
                                              preferred_element_type=F32).astype(BF16)

    @pl.when(step == 0)
    def _():
        for j in range(n_cols // proj_tn):
            project(x_cur_ref, slot, j)

    def group(name, width):
        return _ColumnGroup(proj_scr, slot, cb[name] * LANES, width)

    q_ref, k_ref, v_ref, z_ref, ga_ref = (group(n, d_wide) for n in ("q", "k", "v", "z", "gate_a"))
    r_ref, bk_ref, bv_ref, gb_ref = (group(n, d_wide) for n in ("br", "bk", "bv", "gate_b"))
    l_ref, g_ref, ba_ref = (group(n, LANES) for n in ("lora", "gd", "ba"))

    @pl.when(t == 0)
    def _():
        sg_scr[...] = sg0_ref[...]
        xq_scr[SUB - hist:SUB, :] = cq_ref[...]
        xk_scr[SUB - hist:SUB, :] = ck_ref[...]
        xv_scr[SUB - hist:SUB, :] = cv_ref[...]
        zero = jnp.zeros((nb, nb), F32)
        for p in range(n_pairs):
            sw_scr[p] = jnp.concatenate([jnp.concatenate([sw0_ref[2 * p], zero], 1),
                                         jnp.concatenate([zero, sw0_ref[2 * p + 1]], 1)], 0)
        xr_scr[SUB - 1:SUB, :] = shr_ref[...]
        xbk_scr[SUB - 1:SUB, :] = shk_ref[...]
        xbv_scr[SUB - 1:SUB, :] = shv_ref[...]
        xl_scr[SUB - 1:SUB, :] = shl_ref[...]
        xg_scr[SUB - 1:SUB, :] = shg_ref[...]

    def shifted(x_ref, hs, c, n_shifts, cols, x):
        if c == 0:
            prev = hs[:, cols]
        else:
            lo = c * c_len - BF16_ROWS
            prev = x_ref[lo:lo + BF16_ROWS, cols].astype(F32)[BF16_ROWS - SUB:]
        xw = jnp.concatenate([prev, x], 0)
        return [xw[SUB - 1 - j:SUB - 1 - j + c_len] for j in range(n_shifts)]

    lane = lax.broadcasted_iota(jnp.int32, (1, LANES), 1)
    m0 = lane < nb
    lora_w = wup_ref.shape[0]

    def per_head_sum(x):
        s_lo = jnp.sum(jnp.where(m0, x, 0.0), -1, keepdims=True)
        s_hi = jnp.sum(jnp.where(m0, 0.0, x), -1, keepdims=True)
        return jnp.where(m0, s_lo, s_hi)

    incl, strict, eye_m, upper = _tri_masks(c_len)
    eye = eye_m.astype(F32)
    ltri = incl.astype(BF16)
    n_sq = max(c_len.bit_length() - 2, 0)
    ra = lax.broadcasted_iota(jnp.int32, (3 * c_len, c2), 0)
    ca = lax.broadcasted_iota(jnp.int32, (3 * c_len, c2), 1)
    ca_t = jnp.where(ca >= c_len, ca - c_len, ca)
    ra_t = jnp.where(ra < c_len, ra - 1, (ra - c_len) % c_len)
    mask_a = ra_t >= ca_t
    rb_i = lax.broadcasted_iota(jnp.int32, (c_len, c2), 0)
    cb_i = lax.broadcasted_iota(jnp.int32, (c_len, c2), 1)
    first_cols = cb_i < c_len
    mask_b = rb_i > jnp.where(first_cols, cb_i, cb_i - c_len)
    ii2 = lax.broadcasted_iota(jnp.int32, (c2, c2), 0)
    jj2 = lax.broadcasted_iota(jnp.int32, (c2, c2), 1)
    eye2 = (ii2 == jj2).astype(F32)
    si = lax.broadcasted_iota(jnp.int32, (LANES, LANES), 0)
    sj = lax.broadcasted_iota(jnp.int32, (LANES, LANES), 1)
    same_head = (si // nb) == (sj // nb)

    st = [dict() for _ in range(n_chunks)]
    sg = [sg_scr[h] for h in range(n_heads)]
    sw = [sw_scr[p] for p in range(n_pairs)]

    def front(c, part):
        d = st[c]
        rows = slice(c * c_len, (c + 1) * c_len)

        def conv_step(name, x_ref, hs, w_ref, grp):
            def run():
                cols = slice(grp * group_w, (grp + 1) * group_w)
                w = w_ref[:, cols]
                x = x_ref[rows, cols].astype(F32)
                acc = x * w[hist:CONV_W, :]
                for j, prev in enumerate(shifted(x_ref, hs, c, hist, cols, x)):
                    acc = acc + prev * w[hist - 1 - j:hist - j, :]
                d[name, grp] = _silu(acc)
            return run

        def mix_step(name, x_ref, hs, mu_ref, grp):
            def run():
                cols = slice(grp * group_w, (grp + 1) * group_w) if x_ref.shape[1] > group_w else slice(None)
                x = x_ref[rows, cols].astype(F32)
                d[name, grp] = x + mu_ref[:, cols] * (shifted(x_ref, hs, c, 1, cols, x)[0] - x)
            return run

        def wide(name, h):
            off = h % per_group * LANES
            return d[name, h // per_group][:, off:off + LANES]

        def gates_g():
            ba = ba_ref[rows, :].astype(F32)
            d["beta_all"] = _sigmoid(ba)
            d["g_all"] = -jnp.exp(alog_ref[...]) * _softplus(ba + dtb_ref[...])

        def gates_r():
            xl = d["xl", 0]
            u = w0_ref[...] + _nn(jnp.tanh(xl[:, :lora_w]), wup_ref[...])
            d["logw"] = -DECAY_SCALE * _sigmoid(u)
            d["a_all"] = _sigmoid(a0_ref[...] + _nn(xl[:, lora_w:], aup_ref[...]))
            d["gate"] = _nn(_sigmoid(d["xg", 0]), gup_ref[...])

        def gdn_prep(h):
            def run():
                ls = slice(h * HEAD_A, (h + 1) * HEAD_A)
                qh, kh, v = wide("qc", h), wide("kc", h), wide("vc", h)
                q = qh * lax.rsqrt(jnp.sum(qh * qh, -1, keepdims=True) + NORM_EPS) * (HEAD_A ** -0.5)
                k = kh * lax.rsqrt(jnp.sum(kh * kh, -1, keepdims=True) + NORM_EPS)
                beta = jnp.sum(jnp.where(lane == h, d["beta_all"], 0.0), -1, keepdims=True)
                g = jnp.sum(jnp.where(lane == h + n_heads, d["g_all"], 0.0), -1, keepdims=True)
                g_row = jnp.sum(jnp.where(eye_m, g, 0.0), 0, keepdims=True)
                gc = jnp.sum(jnp.where(incl, g_row, 0.0), 1, keepdims=True)
                gc_row = jnp.sum(jnp.where(upper, g, 0.0), 0, keepdims=True)
                eg = jnp.exp(gc)
                g_last = gc[c_len - 1:c_len, :]
                kb = k * beta
                d["g", h] = {
                    "decay": jnp.where(incl, jnp.exp(jnp.where(incl, gc - gc_row, 0.0)), 0.0),
                    "k": k, "kbq": jnp.concatenate([kb, q], 0),
                    "rhs": jnp.concatenate([v * beta, kb * eg], -1), "qe": q * eg,
                    "kd": k * jnp.exp(g_last - gc), "e_last": jnp.exp(g_last)}
            return run

        def rwkv_prep(p):
            def run():
                ls = slice(p * LANES, (p + 1) * LANES)
                r_p, k_p, v_p, a_p = wide("xr", p), wide("xk", p), wide("xv", p), d["a_all"][:, ls]
                kkv = k_p * kk_ref[:, ls]
                kk = kkv * lax.rsqrt(per_head_sum(kkv * kkv) + NORM_EPS)
                k2 = k_p * (1.0 + (a_p - 1.0) * ka_ref[:, ls])
                bvec = kk * a_p
                lw = d["logw"][:, ls]
                cl = _cumsum_rows(ltri, lw)
                cl_last = cl[c_len - 1:c_len, :]
                e_neg = jnp.exp(-cl)
                rt = r_p * jnp.exp(cl)
                kt = k2 * e_neg
                bt = bvec * e_neg
                at = -kk * jnp.exp(cl - lw)
                e_tail = jnp.exp(cl_last - cl)
                at0 = jnp.where(m0, at, 0.0)
                at1 = jnp.where(m0, 0.0, at)
                d["r", p] = {
                    "bonus": per_head_sum(r_p * k2 * rk_ref[:, ls]) * v_p, "v": v_p, "rt": rt,
                    "lhs_a": jnp.concatenate([at0, jnp.where(m0, rt, 0.0), jnp.where(m0, 0.0, rt)], 0),
                    "rhs_a": jnp.concatenate([bt, kt], 0), "at1": at1, "rhs_b": jnp.concatenate([kt, bt], 0),
                    "a01": jnp.concatenate([at0, at1], 0),
                    "v10": jnp.concatenate([jnp.where(m0, 0.0, v_p), jnp.where(m0, v_p, 0.0)], 0),
                    "bk_bar": jnp.concatenate([bvec * e_tail, k2 * e_tail], 0), "w_last": jnp.exp(cl_last)}
            return run

        if part == "g":
            steps = [(1, gates_g)]
            for grp in range(n_heads // per_group):
                steps += [(2, conv_step("qc", q_ref, xq_scr, wq_ref, grp)),
                          (2, conv_step("kc", k_ref, xk_scr, wk_ref, grp)),
                          (2, conv_step("vc", v_ref, xv_scr, wv_ref, grp))]
                steps += [(3, gdn_prep(h)) for h in range(grp * per_group, (grp + 1) * per_group)]
            return steps
        steps = [(1, mix_step("xl", l_ref, xl_scr, mul_ref, 0)), (1, mix_step("xg", g_ref, xg_scr, mug_ref, 0)),
                 (7, gates_r)]
        for grp in range(n_heads // per_group):
            steps += [(1, mix_step("xr", r_ref, xr_scr, mur_ref, grp)), (1, mix_step("xk", bk_ref, xbk_scr, muk_ref, grp)),
                      (1, mix_step("xv", bv_ref, xbv_scr, muv_ref, grp))]
            steps += [(4, rwkv_prep(h)) for h in range(grp * per_group, (grp + 1) * per_group)]
        return steps

    def mxu(c, part):
        d = st[c]
        g_side = part == "g"

        def gram(i):
            def run():
                a, b = d.get(("g", i)), d.get(("r", i))
                if g_side:
                    gm = _nt(a["kbq"], a["k"])
                    a["qk"] = gm[c_len:] * a["decay"]
                    a["p"] = jnp.where(strict, -gm[:c_len] * a["decay"], 0.0)
                    a["t"] = eye + a["p"]
                    return
                g_a = jnp.where(mask_a, _nt(b["lhs_a"], b["rhs_a"]), 0.0)
                g_b = jnp.where(mask_b, _nt(b["at1"], b["rhs_b"]), 0.0)
                top = g_a[:c_len]
                b["rbk"] = g_a[c_len:]
                b["p"] = jnp.concatenate([jnp.where(first_cols, top, 0.0), jnp.where(first_cols, 0.0, g_b)], 0)
                b["t"] = eye2 + b["p"]
                b["ak"] = jnp.concatenate([jnp.where(first_cols, 0.0, top), jnp.where(first_cols, g_b, 0.0)], 0)
            return run

        def live_rows(m, lo, blocks):
            return m if lo == 0 else jnp.concatenate([m[b * c_len + lo:(b + 1) * c_len] for b in range(blocks)], 0)

        def zero_rows(k):
            return min(2 ** k // BF16_ROWS * BF16_ROWS, c_len)

        def padded(res, lo, blocks):
            if lo == 0:
                return res
            zero = jnp.zeros((lo, res.shape[1]), F32)
            n = c_len - lo
            return jnp.concatenate([x for b in range(blocks) for x in (zero, res[b * n:(b + 1) * n])], 0)

        def added(t_, add, lo, blocks):
            if lo == 0:
                return t_ + add
            n = c_len - lo
            return jnp.concatenate([x for b in range(blocks)
                                    for x in (t_[b * c_len:b * c_len + lo],
                                              t_[b * c_len + lo:(b + 1) * c_len] + add[b * n:(b + 1) * n])], 0)

        def power_step(i, k, with_t, with_p):
            def run():
                for m, blocks in (((d["g", i], 1),) if g_side else ((d["r", i], 2),)):
                    lo_t, lo_p = zero_rows(k), zero_rows(k + 1)
                    lhs = ([live_rows(m["t"], lo_t, blocks)] if with_t else []) \
                        + ([live_rows(m["p"], lo_p, blocks)] if with_p else [])
                    res = _nn(lhs[0] if len(lhs) == 1 else jnp.concatenate(lhs, 0), m["p"])
                    n_t = blocks * (c_len - lo_t) if with_t else 0
                    if with_t:
                        m["t"] = added(m["t"], res[:n_t], lo_t, blocks)
                    if with_p:
                        m["p"] = padded(res[n_t:], lo_p, blocks)
            return run

        def wy(i):
            def run():
                a, b = d.get(("g", i)), d.get(("r", i))
                if g_side:
                    sol = _nn(a["t"], a["rhs"])
                    a["u"] = sol[:, :HEAD_A]
                    a["wq"] = jnp.concatenate([sol[:, HEAD_A:], a["qe"]], 0)
                    return
                akv = _nn(b["ak"], b["v10"])
                sol_r = _nn(b["t"], jnp.concatenate([b["a01"], akv], 1))
                b["wr"] = jnp.concatenate([sol_r[:c_len, :LANES] + sol_r[c_len:, :LANES], b["rt"]], 0)
                b["u"] = sol_r[:c_len, LANES:] + sol_r[c_len:, LANES:]
            return run

        def read_state(i):
            def run():
                a, b = d.get(("g", i)), d.get(("r", i))
                if g_side:
                    a["ws"] = _nn(a["wq"], sg[i])
                else:
                    b["ws"] = _nt(b["wr"], sw[i])
            return run

        def update_state(i):
            def run():
                a, b = d.get(("g", i)), d.get(("r", i))
                if g_side:
                    a["v_new"] = a["u"] - a["ws"][:c_len]
                    sg[i] = sg[i] * a["e_last"] + _tn(a["kd"], a["v_new"])
                else:
                    b["pv"] = jnp.concatenate([b["u"] + b["ws"][:c_len], b["v"]], 0)
                    sw[i] = sw[i] * b["w_last"] + jnp.where(same_head, _tn(b["pv"], b["bk_bar"]), 0.0)
            return run

        def outputs(i):
            def run():
                a, b = d.get(("g", i)), d.get(("r", i))
                if g_side:
                    a["o"] = a["ws"][c_len:] + _nn(a["qk"], a["v_new"])
                else:
                    y_ = _nn(b["rbk"], b["pv"])
                    b["y"] = b["ws"][c_len:] + jnp.where(m0, y_[:c_len], y_[c_len:])
            return run

        heads = range(n_heads)
        steps = [(6, gram(i)) for i in heads]
        for k in range(n_sq + 1):
            if n_sq > 0:
                steps += [(3 if k in (0, n_sq) else 6, power_step(i, k, k > 0, k < n_sq)) for i in heads]
        steps += [(5, wy(i)) for i in heads] + [(4, read_state(i)) for i in heads]
        return steps + [(4, update_state(i)) for i in heads] + [(3, outputs(i)) for i in heads]

    def back(c):
        d = st[c]
        rows = slice(c * c_len, (c + 1) * c_len)

        def head(h):
            def run():
                ls = slice(h * LANES, (h + 1) * LANES)
                o = d["g", h]["o"]
                o = o * lax.rsqrt(jnp.mean(o * o, -1, keepdims=True) + NORM_EPS) * ng_ref[...]
                z = z_ref[rows, ls].astype(F32)
                o_a = o * _silu(z)
                y = d["r", h]["y"]
                mu = per_head_sum(y) * (1.0 / nb)
                yc = y - mu
                var = per_head_sum(yc * yc) * (1.0 / nb)
                y = yc * lax.rsqrt(var + GN_EPS) * lng_ref[:, ls] + lnb_ref[:, ls] + d["r", h]["bonus"]
                o_b = y * d["gate"][:, ls]
                o_ref[rows, ls] = (_sigmoid(ga_ref[rows, ls].astype(F32)) * o_a
                                   + _sigmoid(gb_ref[rows, ls].astype(F32)) * o_b).astype(o_ref.dtype)
            return run

        return [(1, head(h)) for h in range(n_heads)]

    def run_merged(*lists):
        order = []
        for k, steps in enumerate(lists):
            total, done = sum(w for w, _ in steps), 0.0
            for i, (w, _) in enumerate(steps):
                order.append(((done + 0.5 * w) / total, k, i))
                done += w
        for _, k, i in sorted(order):
            lists[k][i][1]()

    n_col_tiles = n_cols // proj_tn

    def project_next(c):
        def tile(j):
            return lambda: project(x_next_ref, 1 - slot, j)
        mine = range(c * n_col_tiles // n_chunks, (c + 1) * n_col_tiles // n_chunks)
        return [(1, tile(j)) for j in mine]

    def halves(steps):
        total, done = sum(w for w, _ in steps), 0.0
        for i, (w, _) in enumerate(steps):
            if done >= total / 2:
                return steps[:i], steps[i:]
            done += w
        return steps, []

    g_half = [halves(mxu(c, "g")) for c in range(n_chunks)]
    r_half = [halves(mxu(c, "r")) for c in range(n_chunks)]
    p_half = [halves(project_next(c)) for c in range(n_chunks)]
    run_merged(front(0, "g"))
    for hp in range(2 * n_chunks + 2):
        c, odd = divmod(hp, 2)
        stages = []
        if not odd:
            if c < n_chunks:
                stages += [g_half[c][0], front(c, "r"), p_half[c][0]]
            if c >= 1:
                stages.append(r_half[c - 1][1])
        else:
            if c < n_chunks:
                stages += [g_half[c][1], r_half[c][0], p_half[c][1]]
            if c + 1 < n_chunks:
                stages.append(front(c + 1, "g"))
            if c >= 1:
                stages.append(back(c - 1))
        run_merged(*[s_ for s_ in stages if s_])

    for h in range(n_heads):
        sg_scr[h] = sg[h]
        sw_scr[h] = sw[h]
    for x_ref, hs in ((q_ref, xq_scr), (k_ref, xk_scr), (v_ref, xv_scr), (r_ref, xr_scr), (bk_ref, xbk_scr),
                      (bv_ref, xbv_scr), (l_ref, xl_scr), (g_ref, xg_scr)):
        hs[...] = x_ref[tc - SUB:tc, :].astype(F32)

    @pl.when(t == pl.num_programs(1) - 1)
    def _():
        tail_ref[...] = proj_scr[slot, tc - BF16_ROWS:tc, :]
        sg_out_ref[...] = sg_scr[...]
        for p in range(n_pairs):
            s_p = sw_scr[p]
            sw_out_ref[2 * p] = s_p[:nb, :nb]
            sw_out_ref[2 * p + 1] = s_p[nb:, nb:]


def _mixers(x, conv_state, s_gdn, shift_state, s_wkv, p, tc, chunk):
    b, t_len, d_model = x.shape
    cb = p["cb"]
    w_in = p["w_in"]
    n_cols = w_in.shape[1]
    n_heads = s_gdn.shape[1]
    d = n_heads * HEAD_A
    n_pairs = s_wkv.shape[1] * HEAD_B // LANES
    assert n_pairs == n_heads and d == n_pairs * LANES
    n_t = t_len // tc
    per = d // LANES
    small = 3 * per
    tiles = x.reshape(b * n_t, tc, d_model)
    last_tile = b * n_t - 1

    def ccol(g):
        return pl.BlockSpec((None, CONV_W - 1, d), lambda bi, ti: (bi, 0, g))

    def wcol(g):
        return pl.BlockSpec((CONV_W, d), lambda bi, ti: (0, g))

    def scol(g):
        return pl.BlockSpec((None, 1, d), lambda bi, ti: (bi, 0, g))

    def sfix(blk):
        return pl.BlockSpec((None, 1, LANES), lambda bi, ti: (bi, 0, blk))

    def mcol(g):
        return pl.BlockSpec((1, d), lambda bi, ti: (0, g))

    def mfix(blk):
        return pl.BlockSpec((1, LANES), lambda bi, ti: (0, blk))

    def full(a):
        return pl.BlockSpec(a.shape, lambda bi, ti: (0,) * a.ndim)

    row = pl.BlockSpec((1, LANES), lambda bi, ti: (0, 0))
    st_g = pl.BlockSpec((None,) + s_gdn.shape[1:], lambda bi, ti: (bi, 0, 0, 0))
    st_w = pl.BlockSpec((None,) + s_wkv.shape[1:], lambda bi, ti: (bi, 0, 0, 0))
    prows = [p[n] for n in ("w0", "a0", "k_k", "k_a", "r_k", "lnx_g", "lnx_b", "w_up", "a_up", "g_up")]
    kern = functools.partial(_mixer_kernel, chunk=chunk, n_chunks=tc // chunk, cb=cb, proj_tn=p["proj_tn"])
    wide = pltpu.VMEM((SUB, d), F32)
    narrow = pltpu.VMEM((SUB, LANES), F32)
    x_tile = (None, tc, d_model)
    return pl.pallas_call(
        kern,
        grid=(b, n_t),
        in_specs=[pl.BlockSpec(x_tile, lambda bi, ti: (bi * n_t + ti, 0, 0)),
                  pl.BlockSpec(x_tile, lambda bi, ti: (jnp.minimum(bi * n_t + ti + 1, last_tile), 0, 0)),
                  pl.BlockSpec(w_in.shape, lambda bi, ti: (0, 0), pipeline_mode=pl.Buffered(1)),
                  ccol(0), ccol(1), ccol(2), st_g, wcol(0), wcol(1), wcol(2), row, row, row,
                  scol(0), scol(1), scol(2), sfix(small), sfix(small + 1),
                  mcol(0), mcol(1), mcol(2), mfix(small), mfix(small + 1)]
                 + [full(a) for a in prows] + [st_w],
        out_specs=[pl.BlockSpec((None, tc, d), lambda bi, ti: (bi, ti, 0)), st_g, st_w,
                   pl.BlockSpec((None, BF16_ROWS, n_cols), lambda bi, ti: (bi, 0, 0))],
        out_shape=[jax.ShapeDtypeStruct((b, t_len, d), BF16),
                   jax.ShapeDtypeStruct(s_gdn.shape, F32), jax.ShapeDtypeStruct(s_wkv.shape, F32),
                   jax.ShapeDtypeStruct((b, BF16_ROWS, n_cols), BF16)],
        scratch_shapes=[pltpu.VMEM((2, tc, n_cols), BF16),
                        pltpu.VMEM((n_heads, HEAD_A, HEAD_A), F32), pltpu.VMEM((n_pairs, LANES, LANES), F32),
                        wide, wide, wide, wide, wide, wide, narrow, narrow],
        compiler_params=pltpu.CompilerParams(
            dimension_semantics=("arbitrary", "arbitrary"), vmem_limit_bytes=VMEM_LIMIT),
        name="mixers",
    )(tiles, tiles, w_in, conv_state, conv_state, conv_state, s_gdn,
      p["conv_w"], p["conv_w"], p["conv_w"], p["alog_row"], p["dtb_row"], p["gdn_norm_g"],
      shift_state, shift_state, shift_state, shift_state, shift_state,
      p["mu_b"], p["mu_b"], p["mu_b"], p["mu_b"], p["mu_b"], *prows, s_wkv)


def _mix_ffn_kernel(x_ref, m_ref, wo_ref, g1_ref, b1_ref, w1_ref, w2_ref, g2_ref, b2_ref,
                    y_ref, *, alpha, ff_chunk):
    mix = jnp.dot(m_ref[...], wo_ref[...], preferred_element_type=F32)
    h = _layer_norm(alpha * x_ref[...] + mix, g1_ref[...], b1_ref[...])
    hb = h.astype(BF16)
    d_ff = w1_ref.shape[1]
    ff = jnp.zeros_like(h)
    for c in range(d_ff // ff_chunk):
        cs = slice(c * ff_chunk, (c + 1) * ff_chunk)
        a = jnp.maximum(jnp.dot(hb, w1_ref[:, cs], preferred_element_type=F32), 0.0)
        ff = ff + jnp.dot((a * a).astype(BF16), w2_ref[cs, :], preferred_element_type=F32)
    y_ref[...] = _layer_norm(alpha * h + ff, g2_ref[...], b2_ref[...])


def _mix_ffn(x2d, merged, w_out, ln1_g, ln1_b, w_ff1, w_ff2, ln2_g, ln2_b, alpha, tm):
    n, d = x2d.shape
    d_ff = w_ff1.shape[1]

    def const(shape):
        return pl.BlockSpec(shape, lambda i: (0, 0), pipeline_mode=pl.Buffered(1))

    tile = pl.BlockSpec((tm, d), lambda i: (i, 0))
    kern = functools.partial(_mix_ffn_kernel, alpha=alpha, ff_chunk=min(d_ff, 1024))
    return pl.pallas_call(
        kern,
        grid=(n // tm,),
        in_specs=[tile, tile, const((d, d)), const((1, d)), const((1, d)),
                  const((d, d_ff)), const((d_ff, d)), const((1, d)), const((1, d))],
        out_specs=tile,
        out_shape=jax.ShapeDtypeStruct((n, d), F32),
        compiler_params=pltpu.CompilerParams(
            dimension_semantics=("parallel",), vmem_limit_bytes=VMEM_LIMIT),
        name="mix_ffn",
    )(x2d, merged, w_out, ln1_g, ln1_b, w_ff1, w_ff2, ln2_g, ln2_b)


def _pick_tile(n, target):
    t = min(n, target)
    while n % t:
        t //= 2
    return t


def _layer(x, conv_buf, s_gdn, shift_buf, s_wkv, p):
    b, t_len, d = x.shape
    n = b * t_len
    cb = p["cb"]
    x2d = x.reshape(n, d)

    chunk = min(CHUNK, t_len)
    tc = _pick_tile(t_len, TIME_TILE)
    assert tc >= BF16_ROWS
    merged, s_gdn_new, s_wkv_new, tail = _mixers(x, conv_buf, s_gdn, shift_buf[:, None, :], s_wkv, p, tc, chunk)

    y = _mix_ffn(x2d, merged.reshape(n, -1), p["w_out"], p["ln1_g"], p["ln1_b"],
                 p["w_ff1"], p["w_ff2"], p["ln2_g"], p["ln2_b"], p["alpha"], _pick_tile(n, 512))

    conv_cols = conv_buf.shape[-1]
    hist = CONV_W - 1
    assert t_len >= hist
    conv_new = tail[:, BF16_ROWS - hist:, :conv_cols].astype(x.dtype)
    last = tail[:, BF16_ROWS - 1]
    shift_new = jnp.concatenate([last[:, cb["br"] * LANES:cb["gate_a"] * LANES],
                                 last[:, cb["lora"] * LANES:cb["ba"] * LANES]], -1).astype(x.dtype)
    return y.reshape(b, t_len, d), conv_new, s_gdn_new, shift_new, s_wkv_new


def _prep_layer(l, depth, w_in, conv_a_w, a_log, dt_bias, gdn_norm_g, mu_b, w0, w_up, a0, a_up, g_up,
                k_k, k_a, r_k, lnx_g, lnx_b, w_out, ln1_g, ln1_b, w_ff1, w_ff2, ln2_g, ln2_b):
    n_heads_a = a_log.shape[1]
    conv_ch = conv_a_w.shape[2]
    d_model = w_out.shape[1]
    v_a = n_heads_a * HEAD_A
    cols_b = mu_b.shape[1]
    small0 = conv_ch + v_a
    small1 = small0 + 2 * n_heads_a
    big_b = cols_b - (w_up.shape[1] + a_up.shape[1] + g_up.shape[1])
    w = w_in[l]
    w_p = jnp.concatenate(
        [w[:, :small0], w[:, small1:small1 + big_b], w[:, small1 + cols_b:], w[:, small1 + big_b:small1 + cols_b],
         jnp.pad(w[:, small0:small1], ((0, 0), (0, LANES - 2 * n_heads_a)))], axis=1).astype(BF16)
    w_p = jnp.pad(w_p, ((0, 0), (0, -w_p.shape[1] % MXU_COLS)))
    blk = lambda c: c // LANES
    cb = {"q": 0, "k": blk(n_heads_a * HEAD_A), "v": blk(2 * n_heads_a * HEAD_A), "z": blk(conv_ch),
          "br": blk(small0)}
    cb["bk"] = cb["br"] + blk(big_b) // 3
    cb["bv"] = cb["bk"] + blk(big_b) // 3
    cb["gate_a"] = cb["br"] + blk(big_b)
    cb["gate_b"] = cb["gate_a"] + blk(d_model)
    cb["lora"] = cb["gate_b"] + blk(d_model)
    cb["gd"] = cb["lora"] + blk(w_up.shape[1] + a_up.shape[1])
    cb["ba"] = cb["gd"] + blk(g_up.shape[1])
    n_blocks = w_p.shape[1] // LANES
    per_mxu = MXU_COLS // LANES
    tn_blocks = max(f for f in range(per_mxu, PROJ_TILE_BLOCKS + 1, per_mxu) if n_blocks % f == 0)

    def lane_row(vec, offset):
        return jnp.zeros((1, LANES), F32).at[0, offset:offset + vec.shape[0]].set(vec)

    row = lambda a: a[l].reshape(1, -1)
    return {
        "cb": cb, "proj_tn": tn_blocks * LANES, "w_in": w_p, "conv_w": conv_a_w[l],
        "alog_row": lane_row(a_log[l], n_heads_a), "dtb_row": lane_row(dt_bias[l], n_heads_a),
        "gdn_norm_g": row(gdn_norm_g), "mu_b": row(mu_b), "w0": row(w0), "a0": row(a0),
        "k_k": row(k_k), "k_a": row(k_a), "r_k": row(r_k), "lnx_g": row(lnx_g), "lnx_b": row(lnx_b),
        "w_up": w_up[l].astype(BF16), "a_up": a_up[l].astype(BF16), "g_up": g_up[l].astype(BF16),
        "w_out": w_out[l].astype(BF16), "ln1_g": row(ln1_g), "ln1_b": row(ln1_b),
        "w_ff1": w_ff1[l].astype(BF16), "w_ff2": w_ff2[l].astype(BF16),
        "ln2_g": row(ln2_g), "ln2_b": row(ln2_b), "alpha": float((2 * depth) ** 0.25),
    }


def kernel(x_prompt, x_sample, state_conv_a, state_gdn, state_shift_b, state_wkv, w_in, conv_a_w, a_log, dt_bias, gdn_norm_g, mu_b, w0, w_up, a0, a_up, g_up, k_k, k_a, r_k, lnx_g, lnx_b, w_out, ln1_g, ln1_b, w_ff1, w_ff2, ln2_g, ln2_b):
    weights = (w_in, conv_a_w, a_log, dt_bias, gdn_norm_g, mu_b, w0, w_up, a0, a_up, g_up,
               k_k, k_a, r_k, lnx_g, lnx_b, w_out, ln1_g, ln1_b, w_ff1, w_ff2, ln2_g, ln2_b)
    depth = w_in.shape[0]
    bp, dtype = x_prompt.shape[0], x_prompt.dtype
    hp, hs = x_prompt, x_sample
    out_p, out_s = [], []
    for l in range(depth):
        p = _prep_layer(l, depth, *weights)
        zeros = lambda ref: jnp.zeros((bp,) + ref.shape[2:], dtype)
        hp, *st_p = _layer(hp, zeros(state_conv_a), zeros(state_gdn), zeros(state_shift_b),
                           zeros(state_wkv), p)
        hs, *st_s = _layer(hs, state_conv_a[l], state_gdn[l], state_shift_b[l], state_wkv[l], p)
        out_p.append(st_p)
        out_s.append(st_s)
    conv_p, gdn_p, shift_p, wkv_p = (jnp.stack([s[i] for s in out_p]) for i in range(4))
    conv_s, gdn_s, shift_s, wkv_s = (jnp.stack([s[i] for s in out_s]) for i in range(4))
    return (hp, hs, conv_p, gdn_p, shift_p, wkv_p, conv_s, gdn_s, shift_s, wkv_s)
```
